```python
import functools
import jax
import jax.numpy as jnp
from jax import lax
import numpy as np

D_MODEL = 2048
BATCH = 16
SEQ = 2048
DEPTH = 1
DEC_BATCH = 32
DEC_SEQ = 1
PAST_LEN = 16384
PAGE_SIZE = 128

SBA_HEAD_DIM = 64
N_SBA_HEADS = D_MODEL // (2 * SBA_HEAD_DIM)
SBA_W = N_SBA_HEADS * SBA_HEAD_DIM
SBA_SCALE = SBA_HEAD_DIM ** -0.5
SBA_BIAS_INIT = -6.0
Q_BLOCK = 128
GDN_HEAD_DIM = 128
N_GDN_HEADS = D_MODEL // (2 * GDN_HEAD_DIM)
GDN_K_W = N_GDN_HEADS * GDN_HEAD_DIM
GDN_V_W = N_GDN_HEADS * GDN_HEAD_DIM
CONV_CH = 2 * GDN_K_W + GDN_V_W
CONV_W = 4
GDN_CHUNK = 64
PLE_DIM = 256
IN_SPLITS = (SBA_W, SBA_W, SBA_W, SBA_W, CONV_CH, N_GDN_HEADS, N_GDN_HEADS, GDN_V_W, D_MODEL, D_MODEL)
IN_W = 4 * SBA_W + CONV_CH + 2 * N_GDN_HEADS + GDN_V_W + 2 * D_MODEL
NORM_EPS = 1e-6

kernel_name = 'stick_breaking_gated_deltanet_hybrid_step'


def rms_norm(x, gain):
    xf = x.astype(jnp.float32)
    y = xf * lax.rsqrt(jnp.mean(xf * xf, axis=-1, keepdims=True) + NORM_EPS)
    return (y * gain.astype(jnp.float32)).astype(x.dtype)


def l2_normalize(x):
    return x * lax.rsqrt(jnp.sum(x * x, axis=-1, keepdims=True) + NORM_EPS)


def stick_breaking_weights(z, mask):
    log_beta = jax.nn.log_sigmoid(z)
    log_surv = jnp.where(mask, jax.nn.log_sigmoid(-z), 0.0)
    later = lax.cumsum(log_surv, axis=z.ndim - 1, reverse=True) - log_surv
    return jnp.where(mask, jnp.exp(log_beta + later), 0.0)


def sba_prompt(q, k, v, bias):
    b, t, h, dh = q.shape
    nb = t // Q_BLOCK
    q_blocks = jnp.moveaxis(q.reshape(b, nb, Q_BLOCK, h, dh), 1, 0)
    q_pos = jnp.arange(t).reshape(nb, Q_BLOCK)
    k_pos = jnp.arange(t)
    bias_f = bias.astype(jnp.float32)[None, :, None, None]

    def block(args):
        qb, qp = args
        z = jnp.einsum('bqhd,bkhd->bhqk', qb, k, preferred_element_type=jnp.float32) * SBA_SCALE + bias_f
        w = stick_breaking_weights(z, k_pos[None, :] < qp[:, None])
        return jnp.einsum('bhqk,bkhd->bqhd', w.astype(v.dtype), v)

    o = lax.map(block, (q_blocks, q_pos))
    return jnp.moveaxis(o, 0, 1).reshape(b, t, h, dh)


def sba_sample(q, k_new, v_new, bias, layer, cache_k, cache_v, page_table):
    db, t = q.shape[0], q.shape[1]
    k_past = cache_k[layer][page_table].reshape(db, -1, N_SBA_HEADS, SBA_HEAD_DIM)
    v_past = cache_v[layer][page_table].reshape(db, -1, N_SBA_HEADS, SBA_HEAD_DIM)
    past = k_past.shape[1]
    z_past = jnp.einsum('bqhd,bkhd->bhqk', q, k_past, preferred_element_type=jnp.float32)
    z_new = jnp.einsum('bqhd,bkhd->bhqk', q, k_new, preferred_element_type=jnp.float32)
    z = (jnp.concatenate([z_past, z_new], axis=-1) * SBA_SCALE
         + bias.astype(jnp.float32)[None, :, None, None])
    q_pos = past + jnp.arange(t)
    k_pos = jnp.arange(past + t)
    w = stick_breaking_weights(z, k_pos[None, :] < q_pos[:, None]).astype(v_new.dtype)
    return (jnp.einsum('bhqk,bkhd->bqhd', w[..., :past], v_past)
            + jnp.einsum('bhqk,bkhd->bqhd', w[..., past:], v_new))


def causal_conv(x, buf, w):
    t = x.shape[1]
    xp = jnp.concatenate([buf.astype(x.dtype), x], axis=1)
    y = xp[:, 0:t] * w[0]
    for i in range(1, CONV_W):
        y = y + xp[:, i:i + t] * w[i]
    return jax.nn.silu(y), xp[:, t:]


def gdn_chunked(q, k, v, g, beta, s0):
    b, t, h, _ = q.shape
    nc = t // GDN_CHUNK

    def chunks(a):
        a = a.reshape((b, nc, GDN_CHUNK, h) + a.shape[3:])
        return jnp.moveaxis(jnp.moveaxis(a, 1, 0), 3, 2)

    qc, kc, vc, gc, bc = chunks(q), chunks(k), chunks(v), chunks(g), chunks(beta)
    cum_g = jnp.cumsum(gc, axis=-1)
    idx = jnp.arange(GDN_CHUNK)
    causal = idx[:, None] >= idx[None, :]
    strict = idx[:, None] > idx[None, :]
    decay = jnp.exp(jnp.where(causal, cum_g[..., :, None] - cum_g[..., None, :], -jnp.inf))
    kb = kc * bc[..., None]
    a_mat = jnp.where(strict, jnp.einsum('nbhid,nbhjd->nbhij', kb, kc) * decay, 0.0)
    lower = a_mat + jnp.eye(GDN_CHUNK, dtype=a_mat.dtype)
    u = lax.linalg.triangular_solve(lower, vc * bc[..., None], left_side=True, lower=True, unit_diagonal=True)
    w = lax.linalg.triangular_solve(lower, kb * jnp.exp(cum_g)[..., None], left_side=True, lower=True, unit_diagonal=True)
    attn = jnp.where(causal, jnp.einsum('nbhid,nbhjd->nbhij', qc, kc) * decay, 0.0)

    def step(s, xs):
        q_i, k_i, u_i, w_i, g_i, attn_i = xs
        delta = u_i - jnp.einsum('bhcd,bhde->bhce', w_i, s)
        o = (jnp.einsum('bhcd,bhde->bhce', q_i * jnp.exp(g_i)[..., None], s)
             + jnp.einsum('bhij,bhje->bhie', attn_i, delta))
        g_last = g_i[..., -1:]
        s = (s * jnp.exp(g_last)[..., None]
             + jnp.einsum('bhcd,bhce->bhde', k_i * jnp.exp(g_last - g_i)[..., None], delta))
        return s, o

    s, o = lax.scan(step, s0, (qc, kc, u, w, cum_g, attn))
    o = jnp.moveaxis(jnp.moveaxis(o, 2, 3), 0, 1).reshape(b, t, h, -1)
    return o, s


def gdn_recurrent(q, k, v, g, beta, s0):
    def step(s, xs):
        q_t, k_t, v_t, g_t, b_t = xs
        s = s * jnp.exp(g_t)[..., None, None]
        delta = b_t[..., None] * (v_t - jnp.einsum('bhk,bhkv->bhv', k_t, s))
        s = s + jnp.einsum('bhk,bhv->bhkv', k_t, delta)
        return s, jnp.einsum('bhk,bhkv->bhv', q_t, s)

    xs = (jnp.moveaxis(q, 1, 0), jnp.moveaxis(k, 1, 0), jnp.moveaxis(v, 1, 0),
          jnp.moveaxis(g, 1, 0), jnp.moveaxis(beta, 1, 0))
    s, o = lax.scan(step, s0, xs)
    return jnp.moveaxis(o, 0, 1), s


def hybrid_layer(x, p, conv_buf, sba_fn, gdn_fn, norm_mix, w_in, sba_bias, conv_w, a_log, dt_bias, gdn_norm,
                 w_up_sba, w_up_gdn, w_o, norm_ple, w_ple_gate, w_ple_proj):
    nb, t, _ = x.shape
    u = rms_norm(x, norm_mix)
    cuts = [int(c) for c in np.cumsum(IN_SPLITS)[:-1]]
    q_s, k_s, v_s, gate_s, qkv_g, a_g, b_g, z_g, m_s, m_g = jnp.split(u @ w_in, cuts, axis=-1)
    q_s = q_s.reshape(nb, t, N_SBA_HEADS, SBA_HEAD_DIM)
    k_s = k_s.reshape(nb, t, N_SBA_HEADS, SBA_HEAD_DIM)
    v_s = v_s.reshape(nb, t, N_SBA_HEADS, SBA_HEAD_DIM)
    y_s = sba_fn(q_s, k_s, v_s, sba_bias).reshape(nb, t, SBA_W) * jax.nn.silu(gate_s)
    qkv_c, conv_new = causal_conv(qkv_g, conv_buf, conv_w)
    q_g, k_g, v_g = jnp.split(qkv_c, [GDN_K_W, 2 * GDN_K_W], axis=-1)
    q_g = l2_normalize(q_g.reshape(nb, t, N_GDN_HEADS, GDN_HEAD_DIM).astype(jnp.float32)) * GDN_HEAD_DIM ** -0.5
    k_g = l2_normalize(k_g.reshape(nb, t, N_GDN_HEADS, GDN_HEAD_DIM).astype(jnp.float32))
    v_g = v_g.reshape(nb, t, N_GDN_HEADS, GDN_HEAD_DIM).astype(jnp.float32)
    g = -jnp.exp(a_log.astype(jnp.float32)) * jax.nn.softplus((a_g + dt_bias).astype(jnp.float32))
    beta = jax.nn.sigmoid(b_g.astype(jnp.float32))
    o_g, s_new = gdn_fn(q_g, k_g, v_g, g, beta)
    y_g = rms_norm(o_g.astype(x.dtype), gdn_norm).reshape(nb, t, GDN_V_W) * jax.nn.silu(z_g)
    merged = jax.nn.sigmoid(m_s) * (y_s @ w_up_sba) + jax.nn.sigmoid(m_g) * (y_g @ w_up_gdn)
    h = x + merged @ w_o
    h = h + jax.nn.sigmoid(rms_norm(h, norm_ple) @ w_ple_gate) * (p @ w_ple_proj)
    return h, (k_s, v_s, conv_new, s_new.astype(x.dtype))


def setup_inputs(seed: int = 0) -> dict:
    key = jax.random.key(seed)
    ks = jax.random.split(key, 24)

    def nrm(i, shape, scale):
        return jax.random.normal(ks[i], shape, jnp.float32) * scale

    n_pages = PAST_LEN // PAGE_SIZE
    n_used = DEC_BATCH * n_pages
    n_phys = n_used + max(1, n_used // 4)
    page_table = jax.random.permutation(ks[0], n_phys)[:n_used].reshape(DEC_BATCH, n_pages).astype(jnp.int32)
    return {
        'x_prompt': nrm(1, (BATCH, SEQ, D_MODEL), 1.0),
        'x_sample': nrm(2, (DEC_BATCH, DEC_SEQ, D_MODEL), 1.0),
        'cache_k': nrm(3, (DEPTH, n_phys, PAGE_SIZE, N_SBA_HEADS, SBA_HEAD_DIM), 1.0),
        'cache_v': nrm(4, (DEPTH, n_phys, PAGE_SIZE, N_SBA_HEADS, SBA_HEAD_DIM), 1.0),
        'state_conv': nrm(5, (DEPTH, DEC_BATCH, CONV_W - 1, CONV_CH), 1.0),
        'state_gdn': nrm(6, (DEPTH, DEC_BATCH, N_GDN_HEADS, GDN_HEAD_DIM, GDN_HEAD_DIM), 0.05),
        'page_table': page_table,
        'p_prompt': nrm(7, (DEPTH, BATCH, SEQ, PLE_DIM), 1.0),
        'p_sample': nrm(8, (DEPTH, DEC_BATCH, DEC_SEQ, PLE_DIM), 1.0),
        'norm_mix': 1.0 + nrm(9, (DEPTH, D_MODEL), 0.02),
        'w_in': nrm(10, (DEPTH, D_MODEL, IN_W), D_MODEL ** -0.5),
        'sba_bias': SBA_BIAS_INIT + nrm(22, (DEPTH, N_SBA_HEADS), 0.1),
        'conv_w': nrm(11, (DEPTH, CONV_W, CONV_CH), CONV_W ** -0.5),
        'gdn_a_log': jnp.log(jax.random.uniform(ks[12], (DEPTH, N_GDN_HEADS), jnp.float32, minval=1.0, maxval=16.0)),
        'gdn_dt_bias': nrm(13, (DEPTH, N_GDN_HEADS), 0.1),
        'gdn_norm': 1.0 + nrm(14, (DEPTH, GDN_HEAD_DIM), 0.02),
        'w_up_sba': nrm(15, (DEPTH, SBA_W, D_MODEL), SBA_W ** -0.5),
        'w_up_gdn': nrm(16, (DEPTH, GDN_V_W, D_MODEL), GDN_V_W ** -0.5),
        'w_o': nrm(17, (DEPTH, D_MODEL, D_MODEL), D_MODEL ** -0.5),
        'norm_ple': 1.0 + nrm(18, (DEPTH, D_MODEL), 0.02),
        'w_ple_gate': nrm(19, (DEPTH, D_MODEL, D_MODEL), D_MODEL ** -0.5),
        'w_ple_proj': nrm(20, (DEPTH, PLE_DIM, D_MODEL), PLE_DIM ** -0.5),
        'norm_final': 1.0 + nrm(21, (D_MODEL,), 0.02),
    }


def reference(x_prompt, x_sample, cache_k, cache_v, state_conv, state_gdn, page_table, p_prompt, p_sample,
              norm_mix, w_in, sba_bias, conv_w, gdn_a_log, gdn_dt_bias, gdn_norm, w_up_sba, w_up_gdn, w_o,
              norm_ple, w_ple_gate, w_ple_proj, norm_final):
    hp, hs = x_prompt, x_sample
    b_p = x_prompt.shape[0]
    kp_l, vp_l, cp_l, sp_l = [], [], [], []
    ks_l, vs_l, cs_l, ss_l = [], [], [], []
    for layer in range(DEPTH):
        lw = (norm_mix[layer], w_in[layer], sba_bias[layer], conv_w[layer], gdn_a_log[layer], gdn_dt_bias[layer],
              gdn_norm[layer], w_up_sba[layer], w_up_gdn[layer], w_o[layer], norm_ple[layer], w_ple_gate[layer],
              w_ple_proj[layer])
        conv0 = jnp.zeros((b_p, CONV_W - 1, CONV_CH), x_prompt.dtype)
        s0 = jnp.zeros((b_p, N_GDN_HEADS, GDN_HEAD_DIM, GDN_HEAD_DIM), jnp.float32)
        hp, (kp, vp, cp, sp) = hybrid_layer(hp, p_prompt[layer], conv0, sba_prompt,
                                            functools.partial(gdn_chunked, s0=s0), *lw)
        sba_s = functools.partial(sba_sample, layer=layer, cache_k=cache_k, cache_v=cache_v, page_table=page_table)
        gdn_s = functools.partial(gdn_recurrent, s0=state_gdn[layer].astype(jnp.float32))
        hs, (ksm, vsm, csm, ssm) = hybrid_layer(hs, p_sample[layer], state_conv[layer], sba_s, gdn_s, *lw)
        kp_l.append(kp)
        vp_l.append(vp)
        cp_l.append(cp)
        sp_l.append(sp)
        ks_l.append(ksm)
        vs_l.append(vsm)
        cs_l.append(csm)
        ss_l.append(ssm)
    y_prompt = rms_norm(hp, norm_final)
    y_sample = rms_norm(hs, norm_final)
    k_prompt = jnp.stack(kp_l)
    v_prompt = jnp.stack(vp_l)
    conv_prompt = jnp.stack(cp_l)
    gdn_prompt = jnp.stack(sp_l)
    k_sample = jnp.stack(ks_l)
    v_sample = jnp.stack(vs_l)
    conv_sample = jnp.stack(cs_l)
    gdn_sample = jnp.stack(ss_l)
    return (y_prompt, y_sample, k_prompt, v_prompt, conv_prompt, gdn_prompt, k_sample, v_sample, conv_sample, gdn_sample)
```

```python
import functools

import jax
import jax.numpy as jnp
from jax import lax
from jax.experimental import pallas as pl
from jax.experimental.pallas import tpu as pltpu

F32 = jnp.float32
BF16 = jnp.bfloat16
NORM_EPS = 1e-6
LANES = 128
SBA_HEAD_DIM = 64
GDN_HEAD_DIM = 128
GDN_CHUNK = 64
CONV_W = 4
VMEM_LIMIT = 56 * 1024 * 1024


def _tile(n, preferred):
    t = min(preferred, n)
    while n % t:
        t //= 2
    return t


def _softplus(x):
    return jnp.maximum(x, 0.0) + jnp.log1p(jnp.exp(-jnp.abs(x)))


def _sigmoid(x):
    return 1.0 / (1.0 + jnp.exp(-x))


def _silu(x):
    return x * _sigmoid(x)


def _rms(x, gain):
    ms = jnp.mean(x * x, axis=-1, keepdims=True)
    return x * lax.rsqrt(ms + NORM_EPS) * gain


def _dot(a, b):
    return jnp.dot(a, b, preferred_element_type=F32)


def _dot_nt(a, b):
    return lax.dot_general(a, b, (((1,), (1,)), ((), ())), preferred_element_type=F32)


def _dot_f32(a, b):
    return jnp.dot(a, b, preferred_element_type=F32, precision=lax.Precision.HIGHEST)


def _split_hi_lo(x):
    hi = x.astype(BF16)
    lo = (x - hi.astype(F32)).astype(BF16)
    return hi, lo


def _rev_cumsum_matrix(n):
    j = lax.broadcasted_iota(jnp.int32, (2 * n, n), 0)
    j = jnp.where(j >= n, j - n, j)
    s = lax.broadcasted_iota(jnp.int32, (2 * n, n), 1)
    return (j >= s).astype(BF16)


def _norm_matmul_kernel(x_ref, g_ref, w_ref, wab_ref, y_ref, ab_ref, u_ref):
    @pl.when(pl.program_id(1) == 0)
    def _():
        u = _rms(x_ref[...], g_ref[...]).astype(BF16)
        u_ref[...] = u
        ab_ref[...] = _dot(u, wab_ref[...])

    y_ref[...] = _dot(u_ref[...], w_ref[...])


def _norm_matmul(x, gain, w, w_ab, *, tm, tn):
    m, d = x.shape
    n = w.shape[1]
    assert m % tm == 0 and n % tn == 0
    return pl.pallas_call(
        _norm_matmul_kernel,
        grid=(m // tm, n // tn),
        in_specs=[
            pl.BlockSpec((tm, d), lambda i, j: (i, 0)),
            pl.BlockSpec((1, d), lambda i, j: (0, 0)),
            pl.BlockSpec((d, tn), lambda i, j: (0, j)),
            pl.BlockSpec((d, LANES), lambda i, j: (0, 0)),
        ],
        out_specs=[
            pl.BlockSpec((tm, tn), lambda i, j: (i, j)),
            pl.BlockSpec((tm, LANES), lambda i, j: (i, 0)),
        ],
        out_shape=[jax.ShapeDtypeStruct((m, n), F32), jax.ShapeDtypeStruct((m, LANES), F32)],
        scratch_shapes=[pltpu.VMEM((tm, d), BF16)],
        compiler_params=pltpu.CompilerParams(
            dimension_semantics=("parallel", "arbitrary"), vmem_limit_bytes=VMEM_LIMIT),
        name="norm_inproj",
    )(x, gain, w, w_ab)


def _sba_prompt_kernel(bias_ref, q_ref, k_ref, v_ref, gate_ref, o_ref, kb_ref, vb_ref, *, tq, tk, scale):
    hd = SBA_HEAD_DIM
    nh = q_ref.shape[-1] // hd
    hp = pl.program_id(1)
    i = pl.program_id(2)

    @pl.when(i == 0)
    def _():
        for h in range(nh):
            kb_ref[h] = k_ref[0, :, h * hd:(h + 1) * hd].astype(BF16)
            vb_ref[h] = v_ref[0, :, h * hd:(h + 1) * hd].astype(BF16)

    q = (q_ref[0] * scale).astype(BF16)
    qs = [q[:, h * hd:(h + 1) * hd] for h in range(nh)]
    bias = [bias_ref[hp * nh + h] for h in range(nh)]
    cmat = _rev_cumsum_matrix(tk)
    row = lax.broadcasted_iota(jnp.int32, (tq, tk), 0)
    col = lax.broadcasted_iota(jnp.int32, (tq, tk), 1)
    q0 = i * tq

    def block(h, j, carry, acc, masked):
        start = pl.multiple_of(j * tk, tk)
        ks = kb_ref[h, pl.ds(start, tk), :]
        vs = vb_ref[h, pl.ds(start, tk), :]
        z = _dot_nt(qs[h], ks) + bias[h]
        sp = _softplus(z)
        if masked:
            keep = (start + col) < (q0 + row)
            sp = jnp.where(keep, sp, 0.0)
        hi, lo = _split_hi_lo(sp)
        cum = _dot(jnp.concatenate([hi, lo], axis=1), cmat)
        w = jnp.exp(z - cum - carry)
        if masked:
            w = jnp.where(keep, w, 0.0)
        acc = acc + _dot(w.astype(BF16), vs)
        carry = carry + jnp.sum(sp, axis=-1, keepdims=True)
        return carry, acc

    state = []
    for h in range(nh):
        state += [jnp.zeros((tq, 1), F32), jnp.zeros((tq, hd), F32)]
    state = tuple(state)

    r = tq // tk
    for jj in range(r - 1, -1, -1):
        new = []
        for h in range(nh):
            new += list(block(h, i * r + jj, state[2 * h], state[2 * h + 1], True))
        state = tuple(new)

    def body(t, st):
        j = i * r - 1 - t
        new = []
        for h in range(nh):
            new += list(block(h, j, st[2 * h], st[2 * h + 1], False))
        return tuple(new)

    state = lax.fori_loop(0, i * r, body, state)
    o = jnp.concatenate([state[2 * h + 1] for h in range(nh)], axis=1)
    o_ref[0] = (o * _silu(gate_ref[0])).astype(o_ref.dtype)


def _sba_prompt(y3, bias, *, n_heads, tq=128, tk=128):
    b, t, _ = y3.shape
    nhp = n_heads * SBA_HEAD_DIM // LANES
    nh = LANES // SBA_HEAD_DIM
    kern = functools.partial(_sba_prompt_kernel, tq=tq, tk=tk, scale=SBA_HEAD_DIM ** -0.5)
    return pl.pallas_call(
        kern,
        grid=(b, nhp, t // tq),
        in_specs=[
            pl.BlockSpec(memory_space=pltpu.SMEM),
            pl.BlockSpec((1, tq, LANES), lambda bi, h, i: (bi, i, h)),
            pl.BlockSpec((1, t, LANES), lambda bi, h, i: (bi, 0, nhp + h)),
            pl.BlockSpec((1, t, LANES), lambda bi, h, i: (bi, 0, 2 * nhp + h)),
            pl.BlockSpec((1, tq, LANES), lambda bi, h, i: (bi, i, 3 * nhp + h)),
        ],
        out_specs=pl.BlockSpec((1, tq, LANES), lambda bi, h, i: (bi, i, h)),
        out_shape=jax.ShapeDtypeStruct((b, t, n_heads * SBA_HEAD_DIM), BF16),
        scratch_shapes=[pltpu.VMEM((nh, t, SBA_HEAD_DIM), BF16), pltpu.VMEM((nh, t, SBA_HEAD_DIM), BF16)],
        compiler_params=pltpu.CompilerParams(
            dimension_semantics=("parallel", "parallel", "arbitrary"), vmem_limit_bytes=VMEM_LIMIT),
        name="sba_prompt",
    )(bias, y3, y3, y3, y3)


def _sba_sample_kernel(pt_ref, q_ref, kn_ref, vn_ref, gate_ref, bias_ref, *rest, pages_per_step, past, scale):
    del pt_ref
    pp = pages_per_step
    k_refs = rest[:pp]
    v_refs = rest[pp:2 * pp]
    o_ref, oacc_ref, carry_ref, qbd_ref = rest[2 * pp:]
    hd = SBA_HEAD_DIM
    nh, width = qbd_ref.shape
    page = k_refs[0].shape[1]
    j = pl.program_id(1)
    head_of_lane = lax.broadcasted_iota(jnp.int32, (nh, width), 1) // hd
    own = head_of_lane == lax.broadcasted_iota(jnp.int32, (nh, width), 0)

    @pl.when(j == 0)
    def _():
        qrow = q_ref[0] * scale
        qbd_ref[...] = jnp.where(own, jnp.broadcast_to(qrow, (nh, width)), 0.0).astype(BF16)
        oacc_ref[...] = jnp.zeros_like(oacc_ref)
        carry_ref[...] = jnp.zeros_like(carry_ref)

    cmat = _rev_cumsum_matrix(page)
    qbd = qbd_ref[...]
    bias = bias_ref[...]
    oacc = oacc_ref[...]
    carry = carry_ref[...]
    for s in range(pp):
        kp = k_refs[s][...].astype(BF16)
        vp = v_refs[s][...].astype(BF16)
        z = _dot(qbd, kp) + bias
        sp = _softplus(z)
        hi, lo = _split_hi_lo(sp)
        cum = _dot(jnp.concatenate([hi, lo], axis=1), cmat)
        w = jnp.exp(z - cum - carry)
        oacc = oacc + _dot_nt(w.astype(BF16), vp)
        carry = carry + jnp.sum(sp, axis=-1, keepdims=True)
    oacc_ref[...] = oacc
    carry_ref[...] = carry

    @pl.when(j == pl.num_programs(1) - 1)
    def _():
        o = jnp.sum(jnp.where(own, oacc, 0.0), axis=0, keepdims=True)
        kn = kn_ref[0].astype(BF16).astype(F32)
        z_new = jnp.sum(qbd.astype(F32) * kn, axis=-1, keepdims=True) + bias
        q_pos = jnp.full(z_new.shape, past, jnp.int32)
        k_pos = jnp.full(z_new.shape, past, jnp.int32)
        w_new = jnp.where(k_pos < q_pos, jnp.exp(z_new - _softplus(z_new)), 0.0)
        w_new = w_new.astype(BF16).astype(F32)
        w_lane = jnp.sum(jnp.where(own, jnp.broadcast_to(w_new, (nh, width)), 0.0), axis=0, keepdims=True)
        o = o + w_lane * vn_ref[0].astype(BF16).astype(F32)
        o_ref[0] = (o * _silu(gate_ref[0])).astype(o_ref.dtype)


def _sba_sample(ys3, bias_col, cache_k, cache_v, page_table, layer, *, n_heads, pages_per_step=8):
    db = ys3.shape[0]
    width = n_heads * SBA_HEAD_DIM
    n_pages = page_table.shape[1]
    page = cache_k.shape[3]
    pp = pages_per_step
    assert n_pages % pp == 0
    kern = functools.partial(_sba_sample_kernel, pages_per_step=pp, past=n_pages * page,
                             scale=SBA_HEAD_DIM ** -0.5)

    def row_spec(col):
        return pl.BlockSpec((1, 1, width), lambda b, j, pt: (b, 0, col))

    def page_spec(s):
        return pl.BlockSpec((None, None, width, page),
                            lambda b, j, pt: (layer, pt[b, n_pages - 1 - (j * pp + s)], 0, 0))

    grid_spec = pltpu.PrefetchScalarGridSpec(
        num_scalar_prefetch=1,
        grid=(db, n_pages // pp),
        in_specs=[row_spec(0), row_spec(1), row_spec(2), row_spec(3),
                  pl.BlockSpec((n_heads, 1), lambda b, j, pt: (0, 0))]
        + [page_spec(s) for s in range(pp)] + [page_spec(s) for s in range(pp)],
        out_specs=pl.BlockSpec((1, 1, width), lambda b, j, pt: (b, 0, 0)),
        scratch_shapes=[pltpu.VMEM((n_heads, width), F32), pltpu.VMEM((n_heads, 1), F32),
                        pltpu.VMEM((n_heads, width), BF16)],
    )
    return pl.pallas_call(
        kern,
        grid_spec=grid_spec,
        out_shape=jax.ShapeDtypeStruct((db, 1, width), BF16),
        compiler_params=pltpu.CompilerParams(
            dimension_semantics=("parallel", "arbitrary"), vmem_limit_bytes=VMEM_LIMIT),
        name="sba_sample",
    )(page_table, ys3, ys3, ys3, ys3, bias_col, *([cache_k] * pp), *([cache_v] * pp))


def _unit_lower_inverse(a):
    n = a.shape[0]
    eye = (lax.broadcasted_iota(jnp.int32, (n, n), 0) == lax.broadcasted_iota(jnp.int32, (n, n), 1)).astype(F32)
    x = -a
    p = eye + x
    span = 2
    while span < n:
        x = _dot_f32(x, x)
        p = p + _dot_f32(p, x)
        span *= 2
    return p


def _gdn_prompt_kernel(xq_ref, xk_ref, xv_ref, zg_ref, ab_ref, cwq_ref, cwk_ref, cwv_ref, alog_ref, dtb_ref,
                       gn_ref, y_ref, s_ref,
                       k_s, q_s, kb_s, vb_s, cg_s, u_s, w_s, qg_s, kd_s, at_s, eg_s, o_s, *, n_heads):
    c = GDN_CHUNK
    dh = GDN_HEAD_DIM
    h = pl.program_id(1)
    t = xq_ref.shape[1]
    nc = t // c
    rows = lax.broadcasted_iota(jnp.int32, (t, dh), 0)
    lane = lax.broadcasted_iota(jnp.int32, (t, dh), 1)

    def conv_silu(x_ref, cw_ref):
        x = x_ref[0]
        cw = cw_ref[...]
        y = None
        for tap in range(CONV_W):
            shift = CONV_W - 1 - tap
            xs = x if shift == 0 else jnp.where(rows >= shift, pltpu.roll(x, shift, axis=0), 0.0)
            term = xs * cw[tap:tap + 1, :]
            y = term if y is None else y + term
        return _silu(y)

    def l2n(x):
        return x * lax.rsqrt(jnp.sum(x * x, axis=-1, keepdims=True) + NORM_EPS)

    q = l2n(conv_silu(xq_ref, cwq_ref)) * (dh ** -0.5)
    k = l2n(conv_silu(xk_ref, cwk_ref))
    v = conv_silu(xv_ref, cwv_ref)

    ab = ab_ref[0]
    g_all = -jnp.exp(alog_ref[...]) * _softplus(ab + dtb_ref[...])
    beta_all = _sigmoid(ab)
    g = jnp.sum(jnp.where(lane == h, g_all, 0.0), axis=-1, keepdims=True)
    beta = jnp.sum(jnp.where(lane == n_heads + h, beta_all, 0.0), axis=-1, keepdims=True)

    cg = jnp.broadcast_to(g, (t, dh))
    pos = rows % c
    step = 1
    while step < c:
        cg = cg + jnp.where(pos >= step, pltpu.roll(cg, step, axis=0), 0.0)
        step *= 2

    k_s[...] = k.reshape(nc, c, dh)
    q_s[...] = q.reshape(nc, c, dh)
    kb_s[...] = (k * beta).reshape(nc, c, dh)
    vb_s[...] = (v * beta).reshape(nc, c, dh)
    cg_s[...] = cg.reshape(nc, c, dh)

    ri = lax.broadcasted_iota(jnp.int32, (c, c), 0)
    ci = lax.broadcasted_iota(jnp.int32, (c, c), 1)
    causal = ri >= ci
    strict = ri > ci
    diag = ri == ci

    def prep(n, _):
        k_c = k_s[n]
        q_c = q_s[n]
        kb_c = kb_s[n]
        cg_c = cg_s[n]
        cg_col = cg_c[:, :c]
        cg_row = jnp.sum(jnp.where(diag, cg_col, 0.0), axis=0, keepdims=True)
        decay = jnp.where(causal, jnp.exp(cg_col - cg_row), 0.0)
        k_b = k_c.astype(BF16)
        a_mat = jnp.where(strict, _dot_nt(kb_c.astype(BF16), k_b) * decay, 0.0)
        attn = jnp.where(causal, _dot_nt(q_c.astype(BF16), k_b) * decay, 0.0)
        tinv = _unit_lower_inverse(a_mat)
        ecg = jnp.exp(cg_c)
        g_last = cg_c[c - 1:c, :]
        u_s[n] = _dot_f32(tinv, vb_s[n])
        w_s[n] = _dot_f32(tinv, kb_c * ecg).astype(BF16)
        qg_s[n] = (q_c * ecg).astype(BF16)
        kd_s[n] = (k_c * jnp.exp(g_last - cg_c)).astype(BF16)
        at_s[n] = attn.astype(BF16)
        eg_s[n] = jnp.exp(g_last)
        return 0

    lax.fori_loop(0, nc, prep, 0)

    def scan(n, s):
        sb = s.astype(BF16)
        delta = u_s[n] - _dot(w_s[n], sb)
        db16 = delta.astype(BF16)
        o_s[n] = _dot(qg_s[n], sb) + _dot(at_s[n], db16)
        upd = lax.dot_general(kd_s[n], db16, (((0,), (0,)), ((), ())), preferred_element_type=F32)
        return s * eg_s[n] + upd

    s_fin = lax.fori_loop(0, nc, scan, jnp.zeros((dh, dh), F32))
    s_ref[0, 0] = s_fin
    o = o_s[...].reshape(t, dh)
    y_ref[0] = (_rms(o, gn_ref[...]) * _silu(zg_ref[0])).astype(y_ref.dtype)


def _gdn_prompt(y3, ab3, conv_w, alog_pad, dtb_pad, gdn_norm, *, n_heads, col0):
    b, t, _ = y3.shape
    dh = GDN_HEAD_DIM
    c = GDN_CHUNK
    nc = t // c
    kern = functools.partial(_gdn_prompt_kernel, n_heads=n_heads)

    def seq_spec(off):
        return pl.BlockSpec((1, t, dh), lambda bi, h: (bi, 0, off + h))

    def cw_spec(off):
        return pl.BlockSpec((CONV_W, dh), lambda bi, h: (0, off + h))

    small = pl.BlockSpec((1, LANES), lambda bi, h: (0, 0))
    chunk_f32 = pltpu.VMEM((nc, c, dh), F32)
    chunk_bf16 = pltpu.VMEM((nc, c, dh), BF16)
    return pl.pallas_call(
        kern,
        grid=(b, n_heads),
        in_specs=[seq_spec(col0), seq_spec(col0 + n_heads), seq_spec(col0 + 2 * n_heads),
                  seq_spec(col0 + 3 * n_heads),
                  pl.BlockSpec((1, t, LANES), lambda bi, h: (bi, 0, 0)),
                  cw_spec(0), cw_spec(n_heads), cw_spec(2 * n_heads), small, small, small],
        out_specs=[pl.BlockSpec((1, t, dh), lambda bi, h: (bi, 0, h)),
                   pl.BlockSpec((1, 1, dh, dh), lambda bi, h: (bi, h, 0, 0))],
        out_shape=[jax.ShapeDtypeStruct((b, t, n_heads * dh), BF16),
                   jax.ShapeDtypeStruct((b, n_heads, dh, dh), F32)],
        scratch_shapes=[chunk_f32, chunk_f32, chunk_f32, chunk_f32, chunk_f32,
                        chunk_f32, chunk_bf16, chunk_bf16, chunk_bf16,
                        pltpu.VMEM((nc, c, c), BF16), pltpu.VMEM((nc, 1, dh), F32),
                        chunk_f32],
        compiler_params=pltpu.CompilerParams(
            dimension_semantics=("parallel", "parallel"), vmem_limit_bytes=VMEM_LIMIT),
        name="gdn_prompt",
    )(y3, y3, y3, y3, ab3, conv_w, conv_w, conv_w, alog_pad, dtb_pad, gdn_norm)


def _gdn_sample_kernel(y_ref, ab_ref, cs_ref, st_ref, cw_ref, alog_ref, dtb_ref, gn_ref,
                       yo_ref, cn_ref, so_ref, *, n_heads, col0):
    dh = GDN_HEAD_DIM
    kw = n_heads * dh
    row = y_ref[0]
    xg = row[:, col0:col0 + 3 * kw]
    zg = row[:, col0 + 3 * kw:col0 + 4 * kw]
    buf = cs_ref[0]
    cw = cw_ref[...]
    yc = buf[0:1] * cw[0:1]
    for tap in range(1, CONV_W - 1):
        yc = yc + buf[tap:tap + 1] * cw[tap:tap + 1]
    yc = yc + xg * cw[CONV_W - 1:CONV_W]
    act = _silu(yc)
    for tap in range(CONV_W - 2):
        cn_ref[0, tap:tap + 1, :] = buf[tap + 1:tap + 2]
    cn_ref[0, CONV_W - 2:CONV_W - 1, :] = xg

    ab = ab_ref[0]
    g_all = -jnp.exp(alog_ref[...]) * _softplus(ab + dtb_ref[...])
    beta_all = _sigmoid(ab)
    eye = lax.broadcasted_iota(jnp.int32, (dh, dh), 0) == lax.broadcasted_iota(jnp.int32, (dh, dh), 1)

    def to_col(r):
        return jnp.sum(jnp.where(eye, jnp.broadcast_to(r, (dh, dh)), 0.0), axis=1, keepdims=True)

    def l2n(x):
        return x * lax.rsqrt(jnp.sum(x * x, axis=-1, keepdims=True) + NORM_EPS)

    gn = gn_ref[...]
    for h in range(n_heads):
        qh = l2n(act[:, h * dh:(h + 1) * dh]) * (dh ** -0.5)
        kh = l2n(act[:, kw + h * dh:kw + (h + 1) * dh])
        vh = act[:, 2 * kw + h * dh:2 * kw + (h + 1) * dh]
        g_h = g_all[:, h:h + 1]
        beta_h = beta_all[:, n_heads + h:n_heads + h + 1]
        s = st_ref[0, h] * jnp.exp(g_h)
        kcol = to_col(kh)
        ks = jnp.sum(kcol * s, axis=0, keepdims=True)
        delta = beta_h * (vh - ks)
        s = s + kcol * delta
        so_ref[0, h] = s
        o = jnp.sum(to_col(qh) * s, axis=0, keepdims=True)
        yo_ref[0, :, h * dh:(h + 1) * dh] = (
            _rms(o, gn) * _silu(zg[:, h * dh:(h + 1) * dh])).astype(yo_ref.dtype)


def _gdn_sample(ys3, ab3, conv_state, gdn_state, conv_w, alog_pad, dtb_pad, gdn_norm, *, n_heads, col0):
    db, _, n = ys3.shape
    dh = GDN_HEAD_DIM
    kw = n_heads * dh
    kern = functools.partial(_gdn_sample_kernel, n_heads=n_heads, col0=col0)
    small = pl.BlockSpec((1, LANES), lambda b: (0, 0))
    return pl.pallas_call(
        kern,
        grid=(db,),
        in_specs=[pl.BlockSpec((1, 1, n), lambda b: (b, 0, 0)),
                  pl.BlockSpec((1, 1, LANES), lambda b: (b, 0, 0)),
                  pl.BlockSpec((1, CONV_W - 1, 3 * kw), lambda b: (b, 0, 0)),
                  pl.BlockSpec((1, n_heads, dh, dh), lambda b: (b, 0, 0, 0)),
                  pl.BlockSpec((CONV_W, 3 * kw), lambda b: (0, 0)),
                  small, small, small],
        out_specs=[pl.BlockSpec((1, 1, kw), lambda b: (b, 0, 0)),
                   pl.BlockSpec((1, CONV_W - 1, 3 * kw), lambda b: (b, 0, 0)),
                   pl.BlockSpec((1, n_heads, dh, dh), lambda b: (b, 0, 0, 0))],
        out_shape=[jax.ShapeDtypeStruct((db, 1, kw), BF16),
                   jax.ShapeDtypeStruct((db, CONV_W - 1, 3 * kw), F32),
                   jax.ShapeDtypeStruct((db, n_heads, dh, dh), F32)],
        compiler_params=pltpu.CompilerParams(
            dimension_semantics=("parallel",), vmem_limit_bytes=VMEM_LIMIT),
        name="gdn_sample",
    )(ys3, ab3, conv_state, gdn_state, conv_w, alog_pad, dtb_pad, gdn_norm)


def _merge_kernel(ys_ref, yg_ref, ms_ref, mg_ref, x_ref, wus_ref, wug_ref, wo_ref, h_ref):
    merged = (_sigmoid(ms_ref[...]) * _dot(ys_ref[...], wus_ref[...])
              + _sigmoid(mg_ref[...]) * _dot(yg_ref[...], wug_ref[...]))
    h_ref[...] = x_ref[...] + _dot(merged.astype(BF16), wo_ref[...])


def _merge(ys, yg, y, x, w_up_sba, w_up_gdn, w_o, *, tm, gate_col):
    m, d = x.shape
    kw = ys.shape[1]

    def resident(shape):
        return pl.BlockSpec(shape, lambda i: (0, 0), pipeline_mode=pl.Buffered(1))

    return pl.pallas_call(
        _merge_kernel,
        grid=(m // tm,),
        in_specs=[pl.BlockSpec((tm, kw), lambda i: (i, 0)),
                  pl.BlockSpec((tm, kw), lambda i: (i, 0)),
                  pl.BlockSpec((tm, d), lambda i: (i, gate_col)),
                  pl.BlockSpec((tm, d), lambda i: (i, gate_col + 1)),
                  pl.BlockSpec((tm, d), lambda i: (i, 0)),
                  resident(w_up_sba.shape), resident(w_up_gdn.shape), resident(w_o.shape)],
        out_specs=pl.BlockSpec((tm, d), lambda i: (i, 0)),
        out_shape=jax.ShapeDtypeStruct((m, d), F32),
        compiler_params=pltpu.CompilerParams(
            dimension_semantics=("parallel",), vmem_limit_bytes=VMEM_LIMIT),
        name="merge_out",
    )(ys, yg, y, y, x, w_up_sba, w_up_gdn, w_o)


def _ple_kernel(h_ref, p_ref, np_ref, nf_ref, wg_ref, wp_ref, o_ref, *, final_norm):
    h = h_ref[...]
    gate = _sigmoid(_dot(_rms(h, np_ref[...]).astype(BF16), wg_ref[...]))
    h = h + gate * _dot(p_ref[...].astype(BF16), wp_ref[...])
    o_ref[...] = _rms(h, nf_ref[...]) if final_norm else h


def _ple(h, p, norm_ple, norm_final, w_gate, w_proj, *, tm, final_norm):
    m, d = h.shape
    pd = p.shape[1]

    def resident(shape):
        return pl.BlockSpec(shape, lambda i: (0, 0), pipeline_mode=pl.Buffered(1))

    return pl.pallas_call(
        functools.partial(_ple_kernel, final_norm=final_norm),
        grid=(m // tm,),
        in_specs=[pl.BlockSpec((tm, d), lambda i: (i, 0)),
                  pl.BlockSpec((tm, pd), lambda i: (i, 0)),
                  resident((1, d)), resident((1, d)), resident(w_gate.shape), resident(w_proj.shape)],
        out_specs=pl.BlockSpec((tm, d), lambda i: (i, 0)),
        out_shape=jax.ShapeDtypeStruct((m, d), F32),
        compiler_params=pltpu.CompilerParams(
            dimension_semantics=("parallel",), vmem_limit_bytes=VMEM_LIMIT),
        name="ple_norm",
    )(h, p, norm_ple, norm_final, w_gate, w_proj)


def _pad_row(v):
    return jnp.zeros((1, LANES), F32).at[0, :v.shape[0]].set(v.astype(F32))


def kernel(x_prompt, x_sample, cache_k, cache_v, state_conv, state_gdn, page_table, p_prompt, p_sample, norm_mix, w_in, sba_bias, conv_w, gdn_a_log, gdn_dt_bias, gdn_norm, w_up_sba, w_up_gdn, w_o, norm_ple, w_ple_gate, w_ple_proj, norm_final):
    depth = w_in.shape[0]
    b, t, d = x_prompt.shape
    db, dt, _ = x_sample.shape
    assert dt == 1
    n_sba = sba_bias.shape[1]
    n_gdn = gdn_a_log.shape[1]
    sba_w = n_sba * SBA_HEAD_DIM
    gdn_w = n_gdn * GDN_HEAD_DIM
    conv_ch = 3 * gdn_w
    ab0 = 4 * sba_w + conv_ch
    n_main = w_in.shape[2] - 2 * n_gdn
    gdn_col0 = 4 * sba_w // LANES
    gate_col = (4 * sba_w + conv_ch + gdn_w) // d
    page = cache_k.shape[2]
    ck = jnp.transpose(cache_k, (0, 1, 3, 4, 2)).reshape(cache_k.shape[0], cache_k.shape[1], sba_w, page)
    cv = jnp.transpose(cache_v, (0, 1, 3, 4, 2)).reshape(cache_v.shape[0], cache_v.shape[1], sba_w, page)

    hp = x_prompt.reshape(b * t, d)
    hs = x_sample.reshape(db, d)
    outs = [[] for _ in range(8)]
    for layer in range(depth):
        w_l = w_in[layer]
        w_main = jnp.concatenate([w_l[:, :ab0], w_l[:, ab0 + 2 * n_gdn:]], axis=1).astype(BF16)
        w_ab = jnp.zeros((d, LANES), BF16).at[:, :2 * n_gdn].set(w_l[:, ab0:ab0 + 2 * n_gdn].astype(BF16))
        gain = norm_mix[layer].reshape(1, d)
        wus = w_up_sba[layer].astype(BF16)
        wug = w_up_gdn[layer].astype(BF16)
        wo = w_o[layer].astype(BF16)
        wpg = w_ple_gate[layer].astype(BF16)
        wpp = w_ple_proj[layer].astype(BF16)
        alog_pad = _pad_row(gdn_a_log[layer])
        dtb_pad = _pad_row(gdn_dt_bias[layer])
        gnorm = gdn_norm[layer].reshape(1, GDN_HEAD_DIM)
        nple = norm_ple[layer].reshape(1, d)
        nfin = norm_final.reshape(1, d)
        last = layer == depth - 1

        y_p, ab_p = _norm_matmul(hp, gain, w_main, w_ab, tm=_tile(b * t, 1024), tn=_tile(n_main, 512))
        y3 = y_p.reshape(b, t, n_main)
        ys_p = _sba_prompt(y3, sba_bias[layer].astype(F32), n_heads=n_sba)
        yg_p, s_p = _gdn_prompt(y3, ab_p.reshape(b, t, LANES), conv_w[layer], alog_pad, dtb_pad, gnorm,
                                n_heads=n_gdn, col0=gdn_col0)
        h_p = _merge(ys_p.reshape(b * t, sba_w), yg_p.reshape(b * t, gdn_w), y_p, hp, wus, wug, wo,
                     tm=_tile(b * t, 256), gate_col=gate_col)
        hp = _ple(h_p, p_prompt[layer].reshape(b * t, -1), nple, nfin, wpg, wpp, tm=_tile(b * t, 512),
                  final_norm=last)
        outs[0].append(y3[:, :, sba_w:2 * sba_w].reshape(b, t, n_sba, SBA_HEAD_DIM))
        outs[1].append(y3[:, :, 2 * sba_w:3 * sba_w].reshape(b, t, n_sba, SBA_HEAD_DIM))
        outs[2].append(y3[:, t - (CONV_W - 1):, 4 * sba_w:4 * sba_w + conv_ch])
        outs[3].append(s_p)

        y_s, ab_s = _norm_matmul(hs, gain, w_main, w_ab, tm=db, tn=_tile(n_main, 512))
        ys3 = y_s.reshape(db, 1, n_main)
        ys_s = _sba_sample(ys3, sba_bias[layer].astype(F32).reshape(n_sba, 1), ck, cv, page_table, layer,
                           n_heads=n_sba, pages_per_step=_tile(page_table.shape[1], 8))
        yg_s, cn_s, s_s = _gdn_sample(ys3, ab_s.reshape(db, 1, LANES), state_conv[layer], state_gdn[layer],
                                      conv_w[layer], alog_pad, dtb_pad, gnorm, n_heads=n_gdn,
                                      col0=4 * sba_w)
        h_s = _merge(ys_s.reshape(db, sba_w), yg_s.reshape(db, gdn_w), y_s, hs, wus, wug, wo,
                     tm=db, gate_col=gate_col)
        hs = _ple(h_s, p_sample[layer].reshape(db, -1), nple, nfin, wpg, wpp, tm=db, final_norm=last)
        outs[4].append(ys3[:, :, sba_w:2 * sba_w].reshape(db, 1, n_sba, SBA_HEAD_DIM))
        outs[5].append(ys3[:, :, 2 * sba_w:3 * sba_w].reshape(db, 1, n_sba, SBA_HEAD_DIM))
        outs[6].append(cn_s)
        outs[7].append(s_s)

    return (hp.reshape(b, t, d), hs.reshape(db, 1, d)) + tuple(jnp.stack(o) for o in outs)
```

```python
import functools

import jax
import jax.numpy as jnp
from jax import lax
from jax.experimental import pallas as pl
from jax.experimental.pallas import tpu as pltpu

F32 = jnp.float32
BF16 = jnp.bfloat16
NORM_EPS = 1e-6
LOG2E = 1.4426950408889634
LANES = 128
SBA_HEAD_DIM = 64
GDN_HEAD_DIM = 128
GDN_CHUNK = 64
GDN_PREP_GROUP = 8
CONV_W = 4
VMEM_LIMIT = 56 * 1024 * 1024


def _tile(n, preferred):
    t = min(preferred, n)
    while n % t:
        t //= 2
    return t


def _softplus(x):
    return jnp.maximum(x, 0.0) + jnp.log1p(jnp.exp(-jnp.abs(x)))


def _sigmoid(x):
    return 0.5 + 0.5 * jnp.tanh(0.5 * x)


def _silu(x):
    h = 0.5 * x
    return h + h * jnp.tanh(h)


def _rms(x, gain):
    ms = jnp.mean(x * x, axis=-1, keepdims=True)
    return x * lax.rsqrt(ms + NORM_EPS) * gain


def _dot(a, b):
    return jnp.dot(a, b, preferred_element_type=F32)


def _dot_nt(a, b):
    return lax.dot_general(a, b, (((1,), (1,)), ((), ())), preferred_element_type=F32)


def _split_hi_lo(x):
    hi = x.astype(BF16)
    lo = (x - hi.astype(F32)).astype(BF16)
    return hi, lo


def _norm_matmul_kernel(x_ref, g_ref, w_ref, wab_ref, y_ref, ab_ref, u_ref):
    @pl.when(pl.program_id(1) == 0)
    def _():
        u = _rms(x_ref[...], g_ref[...]).astype(BF16)
        u_ref[...] = u
        ab_ref[...] = _dot(u, wab_ref[...])

    y_ref[...] = _dot(u_ref[...], w_ref[...])


def _norm_matmul(x, gain, w, w_ab, *, tm, tn):
    m, d = x.shape
    n = w.shape[1]
    assert m % tm == 0 and n % tn == 0
    return pl.pallas_call(
        _norm_matmul_kernel,
        grid=(m // tm, n // tn),
        in_specs=[
            pl.BlockSpec((tm, d), lambda i, j: (i, 0)),
            pl.BlockSpec((1, d), lambda i, j: (0, 0)),
            pl.BlockSpec((d, tn), lambda i, j: (0, j)),
            pl.BlockSpec((d, LANES), lambda i, j: (0, 0)),
        ],
        out_specs=[
            pl.BlockSpec((tm, tn), lambda i, j: (i, j)),
            pl.BlockSpec((tm, LANES), lambda i, j: (i, 0)),
        ],
        out_shape=[jax.ShapeDtypeStruct((m, n), F32), jax.ShapeDtypeStruct((m, LANES), F32)],
        scratch_shapes=[pltpu.VMEM((tm, d), BF16)],
        compiler_params=pltpu.CompilerParams(
            dimension_semantics=("parallel", "arbitrary"), vmem_limit_bytes=VMEM_LIMIT),
        name="norm_inproj",
    )(x, gain, w, w_ab)


def _softplus2(y):
    return jnp.maximum(y, 0.0) + jnp.log2(1.0 + jnp.exp2(-jnp.abs(y)))


def _cumsum_total_matrix(n):
    j = lax.broadcasted_iota(jnp.int32, (2 * n, 2 * n), 0)
    j = jnp.where(j >= n, j - n, j)
    s = lax.broadcasted_iota(jnp.int32, (2 * n, 2 * n), 1)
    return ((j >= s) | (s >= n)).astype(BF16)


def _sba_prompt_kernel(bias_ref, q_ref, k_ref, v_ref, gate_ref, o_ref, kb_ref, vb_ref, *, tq, tk, scale):
    hd = SBA_HEAD_DIM
    nh = q_ref.shape[-1] // hd
    g = pl.program_id(1)
    i = pl.program_id(2)

    @pl.when(i == 0)
    def _():
        for h in range(nh):
            kb_ref[h] = k_ref[0, :, h * hd:(h + 1) * hd].astype(BF16)
            vb_ref[h] = v_ref[0, :, h * hd:(h + 1) * hd].astype(BF16)

    q = (q_ref[0] * (scale * LOG2E)).astype(BF16)
    qs = [q[:, h * hd:(h + 1) * hd] for h in range(nh)]
    bias = [bias_ref[g * nh + h] * LOG2E for h in range(nh)]
    cmat = _cumsum_total_matrix(tk)
    row = lax.broadcasted_iota(jnp.int32, (tq, tk), 0)
    col = lax.broadcasted_iota(jnp.int32, (tq, tk), 1)
    q0 = i * tq

    def blocks(j_hi, state, masked):
        carry, acc = state
        heads = range(nh)
        tiles = [(n, h) for n in range(r) for h in heads]
        start = [pl.multiple_of((j_hi - n) * tk, tk) for n in range(r)]
        y = {(n, h): _dot_nt(qs[h], kb_ref[h, pl.ds(start[n], tk), :]) + bias[h] for n, h in tiles}
        sp = {nh_: _softplus2(y[nh_]) for nh_ in tiles}
        if masked:
            keep = [(start[n] + col) < (q0 + row) for n in range(r)]
            sp = {(n, h): jnp.where(keep[n], sp[n, h], 0.0) for n, h in tiles}
        ct = {nh_: _dot(jnp.concatenate(_split_hi_lo(sp[nh_]), axis=1), cmat) for nh_ in tiles}
        w = {}
        for n in range(r):
            for h in heads:
                x = jnp.exp2(y[n, h] - ct[n, h][:, :tk] - carry[h])
                w[n, h] = (jnp.where(keep[n], x, 0.0) if masked else x).astype(BF16)
            carry = [carry[h] + ct[n, h][:, tk:] for h in heads]
        pv = {(n, h): _dot(w[n, h], vb_ref[h, pl.ds(start[n], tk), :]) for n, h in tiles}
        for n in range(r):
            acc = [acc[h] + pv[n, h] for h in heads]
        return carry, acc

    state = ([jnp.zeros((tq, tk), F32) for _ in range(nh)], [jnp.zeros((tq, hd), F32) for _ in range(nh)])

    r = tq // tk
    state = blocks(i * r + r - 1, state, True)
    state = lax.fori_loop(0, i, lambda t, st: blocks((i - t) * r - 1, st, False), state)
    o = jnp.concatenate(state[1], axis=1)
    o_ref[0] = (o * _silu(gate_ref[0])).astype(o_ref.dtype)


def _sba_prompt(y3, bias, *, n_heads, tq=256, tk=128, width=256):
    b, t, _ = y3.shape
    width = _tile(n_heads * SBA_HEAD_DIM, width)
    ng = n_heads * SBA_HEAD_DIM // width
    nh = width // SBA_HEAD_DIM
    tq = _tile(t, tq)
    kern = functools.partial(_sba_prompt_kernel, tq=tq, tk=tk, scale=SBA_HEAD_DIM ** -0.5)
    return pl.pallas_call(
        kern,
        grid=(b, ng, t // tq),
        in_specs=[
            pl.BlockSpec(memory_space=pltpu.SMEM),
            pl.BlockSpec((1, tq, width), lambda bi, h, i: (bi, i, h)),
            pl.BlockSpec((1, t, width), lambda bi, h, i: (bi, 0, ng + h)),
            pl.BlockSpec((1, t, width), lambda bi, h, i: (bi, 0, 2 * ng + h)),
            pl.BlockSpec((1, tq, width), lambda bi, h, i: (bi, i, 3 * ng + h)),
        ],
        out_specs=pl.BlockSpec((1, tq, width), lambda bi, h, i: (bi, i, h)),
        out_shape=jax.ShapeDtypeStruct((b, t, n_heads * SBA_HEAD_DIM), BF16),
        scratch_shapes=[pltpu.VMEM((nh, t, SBA_HEAD_DIM), BF16), pltpu.VMEM((nh, t, SBA_HEAD_DIM), BF16)],
        compiler_params=pltpu.CompilerParams(
            dimension_semantics=("parallel", "parallel", "arbitrary"), vmem_limit_bytes=VMEM_LIMIT),
        name="sba_prompt",
    )(bias, y3, y3, y3, y3)


def _sba_sample_kernel(pt_ref, q_ref, kn_ref, vn_ref, gate_ref, bias_ref, *rest, pages_per_step, past, scale):
    del pt_ref
    pp = pages_per_step
    k_refs = rest[:pp]
    v_refs = rest[pp:2 * pp]
    o_ref, oacc_ref, carry_ref, qbd_ref = rest[2 * pp:]
    hd = SBA_HEAD_DIM
    nh, width = qbd_ref.shape
    page = k_refs[0].shape[1]
    j = pl.program_id(1)
    head_of_lane = lax.broadcasted_iota(jnp.int32, (nh, width), 1) // hd
    own = head_of_lane == lax.broadcasted_iota(jnp.int32, (nh, width), 0)

    @pl.when(j == 0)
    def _():
        qrow = q_ref[0] * scale
        qbd_ref[...] = jnp.where(own, jnp.broadcast_to(qrow, (nh, width)), 0.0).astype(BF16)
        oacc_ref[...] = jnp.zeros_like(oacc_ref)
        carry_ref[...] = jnp.zeros_like(carry_ref)

    cmat = _cumsum_total_matrix(page)
    qbd = qbd_ref[...]
    bias = bias_ref[...]
    carry = carry_ref[...]
    slots = range(pp)
    z = [_dot(qbd, k_refs[s][...].astype(BF16)) + bias for s in slots]
    sp = [_softplus(z[s]) for s in slots]
    ct = [_dot(jnp.concatenate(_split_hi_lo(sp[s]), axis=1), cmat) for s in slots]
    w = []
    for s in slots:
        w.append(jnp.exp(z[s] - ct[s][:, :page] - carry).astype(BF16))
        carry = carry + ct[s][:, page:]
    pv = [_dot_nt(w[s], v_refs[s][...].astype(BF16)) for s in slots]
    oacc = oacc_ref[...]
    for s in slots:
        oacc = oacc + pv[s]
    oacc_ref[...] = oacc
    carry_ref[...] = carry

    @pl.when(j == pl.num_programs(1) - 1)
    def _():
        o = jnp.sum(jnp.where(own, oacc, 0.0), axis=0, keepdims=True)
        kn = kn_ref[0].astype(BF16).astype(F32)
        z_new = jnp.sum(qbd.astype(F32) * kn, axis=-1, keepdims=True) + bias
        q_pos = jnp.full(z_new.shape, past, jnp.int32)
        k_pos = jnp.full(z_new.shape, past, jnp.int32)
        w_new = jnp.where(k_pos < q_pos, jnp.exp(z_new - _softplus(z_new)), 0.0)
        w_new = w_new.astype(BF16).astype(F32)
        w_lane = jnp.sum(jnp.where(own, jnp.broadcast_to(w_new, (nh, width)), 0.0), axis=0, keepdims=True)
        o = o + w_lane * vn_ref[0].astype(BF16).astype(F32)
        o_ref[0] = (o * _silu(gate_ref[0])).astype(o_ref.dtype)


def _sba_sample(ys3, bias_col, cache_k, cache_v, page_table, layer, *, n_heads, pages_per_step=8):
    db = ys3.shape[0]
    width = n_heads * SBA_HEAD_DIM
    n_pages = page_table.shape[1]
    page = cache_k.shape[3]
    pp = pages_per_step
    assert n_pages % pp == 0
    kern = functools.partial(_sba_sample_kernel, pages_per_step=pp, past=n_pages * page,
                             scale=SBA_HEAD_DIM ** -0.5)

    def row_spec(col):
        return pl.BlockSpec((1, 1, width), lambda b, j, pt: (b, 0, col))

    def page_spec(s):
        return pl.BlockSpec((None, None, width, page),
                            lambda b, j, pt: (layer, pt[b, n_pages - 1 - (j * pp + s)], 0, 0))

    grid_spec = pltpu.PrefetchScalarGridSpec(
        num_scalar_prefetch=1,
        grid=(db, n_pages // pp),
        in_specs=[row_spec(0), row_spec(1), row_spec(2), row_spec(3),
                  pl.BlockSpec((n_heads, 1), lambda b, j, pt: (0, 0))]
        + [page_spec(s) for s in range(pp)] + [page_spec(s) for s in range(pp)],
        out_specs=pl.BlockSpec((1, 1, width), lambda b, j, pt: (b, 0, 0)),
        scratch_shapes=[pltpu.VMEM((n_heads, width), F32), pltpu.VMEM((n_heads, page), F32),
                        pltpu.VMEM((n_heads, width), BF16)],
    )
    return pl.pallas_call(
        kern,
        grid_spec=grid_spec,
        out_shape=jax.ShapeDtypeStruct((db, 1, width), BF16),
        compiler_params=pltpu.CompilerParams(
            dimension_semantics=("parallel", "arbitrary"), vmem_limit_bytes=VMEM_LIMIT),
        name="sba_sample",
    )(page_table, ys3, ys3, ys3, ys3, bias_col, *([cache_k] * pp), *([cache_v] * pp))


def _bdot_nt(a, b):
    return jnp.einsum('gmk,gnk->gmn', a, b, preferred_element_type=F32)


def _bdot_split(a, b):
    a_hi, a_lo = _split_hi_lo(a)
    b_hi, b_lo = _split_hi_lo(b)
    lhs = jnp.concatenate([a_hi, a_lo, a_hi], axis=2)
    rhs = jnp.concatenate([b_hi, b_hi, b_lo], axis=1)
    return jnp.einsum('gmk,gkn->gmn', lhs, rhs, preferred_element_type=F32)


def _unit_lower_inverse(a):
    n = a.shape[-1]
    eye = (lax.broadcasted_iota(jnp.int32, (n, n), 0) == lax.broadcasted_iota(jnp.int32, (n, n), 1)).astype(F32)
    x = -a
    p = eye + x
    span = 2
    while span < n:
        x = _bdot_split(x, x)
        p = p + _bdot_split(p, x)
        span *= 2
    return p


def _gdn_prompt_kernel(xq_ref, xk_ref, xv_ref, zg_ref, ab_ref, cwq_ref, cwk_ref, cwv_ref, alog_ref, dtb_ref,
                       gn_ref, y_ref, s_ref,
                       k_s, q_s, kb_s, vb_s, cg_s, u_s, w_s, qg_s, kd_s, at_s, eg_s, o_s, *, n_heads):
    c = GDN_CHUNK
    dh = GDN_HEAD_DIM
    h = pl.program_id(1)
    t = xq_ref.shape[1]
    nc = t // c
    group = _tile(nc, GDN_PREP_GROUP)
    rows = lax.broadcasted_iota(jnp.int32, (t, dh), 0)
    lane = lax.broadcasted_iota(jnp.int32, (t, dh), 1)

    def conv_silu(x_ref, cw_ref):
        x = x_ref[0]
        cw = cw_ref[...]
        y = None
        for tap in range(CONV_W):
            shift = CONV_W - 1 - tap
            xs = x if shift == 0 else jnp.where(rows >= shift, pltpu.roll(x, shift, axis=0), 0.0)
            term = xs * cw[tap:tap + 1, :]
            y = term if y is None else y + term
        return _silu(y)

    def l2n(x):
        return x * lax.rsqrt(jnp.sum(x * x, axis=-1, keepdims=True) + NORM_EPS)

    q = l2n(conv_silu(xq_ref, cwq_ref)) * (dh ** -0.5)
    k = l2n(conv_silu(xk_ref, cwk_ref))
    v = conv_silu(xv_ref, cwv_ref)

    ab = ab_ref[0]
    g_all = -jnp.exp(alog_ref[...]) * _softplus(ab + dtb_ref[...])
    beta_all = _sigmoid(ab)
    g = jnp.sum(jnp.where(lane == h, g_all, 0.0), axis=-1, keepdims=True)
    beta = jnp.sum(jnp.where(lane == n_heads + h, beta_all, 0.0), axis=-1, keepdims=True)

    cg = jnp.broadcast_to(g, (t, dh))
    pos = rows % c
    step = 1
    while step < c:
        cg = cg + jnp.where(pos >= step, pltpu.roll(cg, step, axis=0), 0.0)
        step *= 2

    k_s[...] = k.reshape(nc, c, dh)
    q_s[...] = q.reshape(nc, c, dh)
    kb_s[...] = (k * beta).reshape(nc, c, dh)
    vb_s[...] = (v * beta).reshape(nc, c, dh)
    cg_s[...] = cg.reshape(nc, c, dh)

    ri = lax.broadcasted_iota(jnp.int32, (c, c), 0)
    ci = lax.broadcasted_iota(jnp.int32, (c, c), 1)
    causal = ri >= ci
    strict = ri > ci
    diag = ri == ci

    def prep(n, _):
        sl = pl.ds(pl.multiple_of(n * group, group), group)
        k_c = k_s[sl]
        q_c = q_s[sl]
        kb_c = kb_s[sl]
        cg_c = cg_s[sl]
        cg_col = cg_c[:, :, :c]
        cg_row = jnp.sum(jnp.where(diag, cg_col, 0.0), axis=1, keepdims=True)
        decay = jnp.where(causal, jnp.exp(cg_col - cg_row), 0.0)
        k_b = k_c.astype(BF16)
        a_mat = jnp.where(strict, _bdot_nt(kb_c.astype(BF16), k_b) * decay, 0.0)
        attn = jnp.where(causal, _bdot_nt(q_c.astype(BF16), k_b) * decay, 0.0)
        tinv = _unit_lower_inverse(a_mat)
        ecg = jnp.exp(cg_c)
        g_last = cg_c[:, c - 1:c, :]
        uw = _bdot_split(tinv, jnp.concatenate([vb_s[sl], kb_c * ecg], axis=2))
        u_s[sl] = uw[:, :, :dh]
        w_s[sl] = uw[:, :, dh:].astype(BF16)
        qg_s[sl] = (q_c * ecg).astype(BF16)
        kd_s[sl] = (k_c * jnp.exp(g_last - cg_c)).astype(BF16)
        at_s[sl] = attn.astype(BF16)
        eg_s[sl] = jnp.exp(g_last)
        return 0

    lax.fori_loop(0, nc // group, prep, 0)

    def scan(n, s):
        sb = s.astype(BF16)
        delta = u_s[n] - _dot(w_s[n], sb)
        db16 = delta.astype(BF16)
        o_s[n] = _dot(qg_s[n], sb) + _dot(at_s[n], db16)
        upd = lax.dot_general(kd_s[n], db16, (((0,), (0,)), ((), ())), preferred_element_type=F32)
        return s * eg_s[n] + upd

    s_fin = lax.fori_loop(0, nc, scan, jnp.zeros((dh, dh), F32))
    s_ref[0, 0] = s_fin
    o = o_s[...].reshape(t, dh)
    y_ref[0] = (_rms(o, gn_ref[...]) * _silu(zg_ref[0])).astype(y_ref.dtype)


def _gdn_prompt(y3, ab3, conv_w, alog_pad, dtb_pad, gdn_norm, *, n_heads, col0):
    b, t, _ = y3.shape
    dh = GDN_HEAD_DIM
    c = GDN_CHUNK
    nc = t // c
    kern = functools.partial(_gdn_prompt_kernel, n_heads=n_heads)

    def seq_spec(off):
        return pl.BlockSpec((1, t, dh), lambda bi, h: (bi, 0, off + h))

    def cw_spec(off):
        return pl.BlockSpec((CONV_W, dh), lambda bi, h: (0, off + h))

    small = pl.BlockSpec((1, LANES), lambda bi, h: (0, 0))
    chunk_f32 = pltpu.VMEM((nc, c, dh), F32)
    chunk_bf16 = pltpu.VMEM((nc, c, dh), BF16)
    return pl.pallas_call(
        kern,
        grid=(b, n_heads),
        in_specs=[seq_spec(col0), seq_spec(col0 + n_heads), seq_spec(col0 + 2 * n_heads),
                  seq_spec(col0 + 3 * n_heads),
                  pl.BlockSpec((1, t, LANES), lambda bi, h: (bi, 0, 0)),
                  cw_spec(0), cw_spec(n_heads), cw_spec(2 * n_heads), small, small, small],
        out_specs=[pl.BlockSpec((1, t, dh), lambda bi, h: (bi, 0, h)),
                   pl.BlockSpec((1, 1, dh, dh), lambda bi, h: (bi, h, 0, 0))],
        out_shape=[jax.ShapeDtypeStruct((b, t, n_heads * dh), BF16),
                   jax.ShapeDtypeStruct((b, n_heads, dh, dh), F32)],
        scratch_shapes=[chunk_f32, chunk_f32, chunk_f32, chunk_f32, chunk_f32,
                        chunk_f32, chunk_bf16, chunk_bf16, chunk_bf16,
                        pltpu.VMEM((nc, c, c), BF16), pltpu.VMEM((nc, 1, dh), F32),
                        chunk_f32],
        compiler_params=pltpu.CompilerParams(
            dimension_semantics=("parallel", "parallel"), vmem_limit_bytes=VMEM_LIMIT),
        name="gdn_prompt",
    )(y3, y3, y3, y3, ab3, conv_w, conv_w, conv_w, alog_pad, dtb_pad, gdn_norm)


def _gdn_sample_kernel(y_ref, ab_ref, cs_ref, st_ref, cw_ref, alog_ref, dtb_ref, gn_ref,
                       yo_ref, cn_ref, so_ref, *, n_heads, col0):
    dh = GDN_HEAD_DIM
    kw = n_heads * dh
    row = y_ref[0]
    xg = row[:, col0:col0 + 3 * kw]
    zg = row[:, col0 + 3 * kw:col0 + 4 * kw]
    buf = cs_ref[0]
    cw = cw_ref[...]
    yc = buf[0:1] * cw[0:1]
    for tap in range(1, CONV_W - 1):
        yc = yc + buf[tap:tap + 1] * cw[tap:tap + 1]
    yc = yc + xg * cw[CONV_W - 1:CONV_W]
    act = _silu(yc)
    for tap in range(CONV_W - 2):
        cn_ref[0, tap:tap + 1, :] = buf[tap + 1:tap + 2]
    cn_ref[0, CONV_W - 2:CONV_W - 1, :] = xg

    ab = ab_ref[0]
    g_all = -jnp.exp(alog_ref[...]) * _softplus(ab + dtb_ref[...])
    beta_all = _sigmoid(ab)
    eye = lax.broadcasted_iota(jnp.int32, (dh, dh), 0) == lax.broadcasted_iota(jnp.int32, (dh, dh), 1)

    def to_col(r):
        return jnp.sum(jnp.where(eye, jnp.broadcast_to(r, (dh, dh)), 0.0), axis=1, keepdims=True)

    def l2n(x):
        return x * lax.rsqrt(jnp.sum(x * x, axis=-1, keepdims=True) + NORM_EPS)

    gn = gn_ref[...]
    for h in range(n_heads):
        qh = l2n(act[:, h * dh:(h + 1) * dh]) * (dh ** -0.5)
        kh = l2n(act[:, kw + h * dh:kw + (h + 1) * dh])
        vh = act[:, 2 * kw + h * dh:2 * kw + (h + 1) * dh]
        g_h = g_all[:, h:h + 1]
        beta_h = beta_all[:, n_heads + h:n_heads + h + 1]
        s = st_ref[0, h] * jnp.exp(g_h)
        kcol = to_col(kh)
        ks = jnp.sum(kcol * s, axis=0, keepdims=True)
        delta = beta_h * (vh - ks)
        s = s + kcol * delta
        so_ref[0, h] = s
        o = jnp.sum(to_col(qh) * s, axis=0, keepdims=True)
        yo_ref[0, :, h * dh:(h + 1) * dh] = (
            _rms(o, gn) * _silu(zg[:, h * dh:(h + 1) * dh])).astype(yo_ref.dtype)


def _gdn_sample(ys3, ab3, conv_state, gdn_state, conv_w, alog_pad, dtb_pad, gdn_norm, *, n_heads, col0):
    db, _, n = ys3.shape
    dh = GDN_HEAD_DIM
    kw = n_heads * dh
    kern = functools.partial(_gdn_sample_kernel, n_heads=n_heads, col0=col0)
    small = pl.BlockSpec((1, LANES), lambda b: (0, 0))
    return pl.pallas_call(
        kern,
        grid=(db,),
        in_specs=[pl.BlockSpec((1, 1, n), lambda b: (b, 0, 0)),
                  pl.BlockSpec((1, 1, LANES), lambda b: (b, 0, 0)),
                  pl.BlockSpec((1, CONV_W - 1, 3 * kw), lambda b: (b, 0, 0)),
                  pl.BlockSpec((1, n_heads, dh, dh), lambda b: (b, 0, 0, 0)),
                  pl.BlockSpec((CONV_W, 3 * kw), lambda b: (0, 0)),
                  small, small, small],
        out_specs=[pl.BlockSpec((1, 1, kw), lambda b: (b, 0, 0)),
                   pl.BlockSpec((1, CONV_W - 1, 3 * kw), lambda b: (b, 0, 0)),
                   pl.BlockSpec((1, n_heads, dh, dh), lambda b: (b, 0, 0, 0))],
        out_shape=[jax.ShapeDtypeStruct((db, 1, kw), BF16),
                   jax.ShapeDtypeStruct((db, CONV_W - 1, 3 * kw), F32),
                   jax.ShapeDtypeStruct((db, n_heads, dh, dh), F32)],
        compiler_params=pltpu.CompilerParams(
            dimension_semantics=("parallel",), vmem_limit_bytes=VMEM_LIMIT),
        name="gdn_sample",
    )(ys3, ab3, conv_state, gdn_state, conv_w, alog_pad, dtb_pad, gdn_norm)


def _merge_kernel(ys_ref, yg_ref, ms_ref, mg_ref, x_ref, wus_ref, wug_ref, wo_ref, h_ref):
    merged = (_sigmoid(ms_ref[...]) * _dot(ys_ref[...], wus_ref[...])
              + _sigmoid(mg_ref[...]) * _dot(yg_ref[...], wug_ref[...]))
    h_ref[...] = x_ref[...] + _dot(merged.astype(BF16), wo_ref[...])


def _merge(ys, yg, y, x, w_up_sba, w_up_gdn, w_o, *, tm, gate_col):
    m, d = x.shape
    kw = ys.shape[1]

    def resident(shape):
        return pl.BlockSpec(shape, lambda i: (0, 0), pipeline_mode=pl.Buffered(1))

    return pl.pallas_call(
        _merge_kernel,
        grid=(m // tm,),
        in_specs=[pl.BlockSpec((tm, kw), lambda i: (i, 0)),
                  pl.BlockSpec((tm, kw), lambda i: (i, 0)),
                  pl.BlockSpec((tm, d), lambda i: (i, gate_col)),
                  pl.BlockSpec((tm, d), lambda i: (i, gate_col + 1)),
                  pl.BlockSpec((tm, d), lambda i: (i, 0)),
                  resident(w_up_sba.shape), resident(w_up_gdn.shape), resident(w_o.shape)],
        out_specs=pl.BlockSpec((tm, d), lambda i: (i, 0)),
        out_shape=jax.ShapeDtypeStruct((m, d), F32),
        compiler_params=pltpu.CompilerParams(
            dimension_semantics=("parallel",), vmem_limit_bytes=VMEM_LIMIT),
        name="merge_out",
    )(ys, yg, y, y, x, w_up_sba, w_up_gdn, w_o)


def _ple_kernel(h_ref, p_ref, np_ref, nf_ref, wg_ref, wp_ref, o_ref, *, final_norm):
    h = h_ref[...]
    gate = _sigmoid(_dot(_rms(h, np_ref[...]).astype(BF16), wg_ref[...]))
    h = h + gate * _dot(p_ref[...].astype(BF16), wp_ref[...])
    o_ref[...] = _rms(h, nf_ref[...]) if final_norm else h


def _ple(h, p, norm_ple, norm_final, w_gate, w_proj, *, tm, final_norm):
    m, d = h.shape
    pd = p.shape[1]

    def resident(shape):
        return pl.BlockSpec(shape, lambda i: (0, 0), pipeline_mode=pl.Buffered(1))

    return pl.pallas_call(
        functools.partial(_ple_kernel, final_norm=final_norm),
        grid=(m // tm,),
        in_specs=[pl.BlockSpec((tm, d), lambda i: (i, 0)),
                  pl.BlockSpec((tm, pd), lambda i: (i, 0)),
                  resident((1, d)), resident((1, d)), resident(w_gate.shape), resident(w_proj.shape)],
        out_specs=pl.BlockSpec((tm, d), lambda i: (i, 0)),
        out_shape=jax.ShapeDtypeStruct((m, d), F32),
        compiler_params=pltpu.CompilerParams(
            dimension_semantics=("parallel",), vmem_limit_bytes=VMEM_LIMIT),
        name="ple_norm",
    )(h, p, norm_ple, norm_final, w_gate, w_proj)


def _pad_row(v):
    return jnp.zeros((1, LANES), F32).at[0, :v.shape[0]].set(v.astype(F32))


def kernel(x_prompt, x_sample, cache_k, cache_v, state_conv, state_gdn, page_table, p_prompt, p_sample, norm_mix, w_in, sba_bias, conv_w, gdn_a_log, gdn_dt_bias, gdn_norm, w_up_sba, w_up_gdn, w_o, norm_ple, w_ple_gate, w_ple_proj, norm_final):
    depth = w_in.shape[0]
    b, t, d = x_prompt.shape
    db, dt, _ = x_sample.shape
    assert dt == 1
    n_sba = sba_bias.shape[1]
    n_gdn = gdn_a_log.shape[1]
    sba_w = n_sba * SBA_HEAD_DIM
    gdn_w = n_gdn * GDN_HEAD_DIM
    conv_ch = 3 * gdn_w
    ab0 = 4 * sba_w + conv_ch
    n_main = w_in.shape[2] - 2 * n_gdn
    gdn_col0 = 4 * sba_w // LANES
    gate_col = (4 * sba_w + conv_ch + gdn_w) // d
    page = cache_k.shape[2]
    ck = jnp.transpose(cache_k, (0, 1, 3, 4, 2)).reshape(cache_k.shape[0], cache_k.shape[1], sba_w, page)
    cv = jnp.transpose(cache_v, (0, 1, 3, 4, 2)).reshape(cache_v.shape[0], cache_v.shape[1], sba_w, page)

    hp = x_prompt.reshape(b * t, d)
    hs = x_sample.reshape(db, d)
    outs = [[] for _ in range(8)]
    for layer in range(depth):
        w_l = w_in[layer]
        w_main = jnp.concatenate([w_l[:, :ab0], w_l[:, ab0 + 2 * n_gdn:]], axis=1).astype(BF16)
        w_ab = jnp.zeros((d, LANES), BF16).at[:, :2 * n_gdn].set(w_l[:, ab0:ab0 + 2 * n_gdn].astype(BF16))
        gain = norm_mix[layer].reshape(1, d)
        wus = w_up_sba[layer].astype(BF16)
        wug = w_up_gdn[layer].astype(BF16)
        wo = w_o[layer].astype(BF16)
        wpg = w_ple_gate[layer].astype(BF16)
        wpp = w_ple_proj[layer].astype(BF16)
        alog_pad = _pad_row(gdn_a_log[layer])
        dtb_pad = _pad_row(gdn_dt_bias[layer])
        gnorm = gdn_norm[layer].reshape(1, GDN_HEAD_DIM)
        nple = norm_ple[layer].reshape(1, d)
        nfin = norm_final.reshape(1, d)
        last = layer == depth - 1

        y_p, ab_p = _norm_matmul(hp, gain, w_main, w_ab, tm=_tile(b * t, 1024), tn=_tile(n_main, 512))
        y3 = y_p.reshape(b, t, n_main)
        ys_p = _sba_prompt(y3, sba_bias[layer].astype(F32), n_heads=n_sba)
        yg_p, s_p = _gdn_prompt(y3, ab_p.reshape(b, t, LANES), conv_w[layer], alog_pad, dtb_pad, gnorm,
                                n_heads=n_gdn, col0=gdn_col0)
        h_p = _merge(ys_p.reshape(b * t, sba_w), yg_p.reshape(b * t, gdn_w), y_p, hp, wus, wug, wo,
                     tm=_tile(b * t, 256), gate_col=gate_col)
        hp = _ple(h_p, p_prompt[layer].reshape(b * t, -1), nple, nfin, wpg, wpp, tm=_tile(b * t, 512),
                  final_norm=last)
        outs[0].append(y3[:, :, sba_w:2 * sba_w].reshape(b, t, n_sba, SBA_HEAD_DIM))
        outs[1].append(y3[:, :, 2 * sba_w:3 * sba_w].reshape(b, t, n_sba, SBA_HEAD_DIM))
        outs[2].append(y3[:, t - (CONV_W - 1):, 4 * sba_w:4 * sba_w + conv_ch])
        outs[3].append(s_p)

        y_s, ab_s = _norm_matmul(hs, gain, w_main, w_ab, tm=db, tn=_tile(n_main, 512))
        ys3 = y_s.reshape(db, 1, n_main)
        ys_s = _sba_sample(ys3, sba_bias[layer].astype(F32).reshape(n_sba, 1), ck, cv, page_table, layer,
                           n_heads=n_sba, pages_per_step=_tile(page_table.shape[1], 8))
        yg_s, cn_s, s_s = _gdn_sample(ys3, ab_s.reshape(db, 1, LANES), state_conv[layer], state_gdn[layer],
                                      conv_w[layer], alog_pad, dtb_pad, gnorm, n_heads=n_gdn,
                                      col0=4 * sba_w)
        h_s = _merge(ys_s.reshape(db, sba_w), yg_s.reshape(db, gdn_w), y_s, hs, wus, wug, wo,
                     tm=db, gate_col=gate_col)
        hs = _ple(h_s, p_sample[layer].reshape(db, -1), nple, nfin, wpg, wpp, tm=db, final_norm=last)
        outs[4].append(ys3[:, :, sba_w:2 * sba_w].reshape(db, 1, n_sba, SBA_HEAD_DIM))
        outs[5].append(ys3[:, :, 2 * sba_w:3 * sba_w].reshape(db, 1, n_sba, SBA_HEAD_DIM))
        outs[6].append(cn_s)
        outs[7].append(s_s)

    return (hp.reshape(b, t, d), hs.reshape(db, 1, d)) + tuple(jnp.stack(o) for o in outs)
```

```python
import functools

import jax
import jax.numpy as jnp
from jax import lax
from jax.experimental import pallas as pl
from jax.experimental.pallas import tpu as pltpu

F32 = jnp.float32
BF16 = jnp.bfloat16
NORM_EPS = 1e-6
LOG2E = 1.4426950408889634
LANES = 128
SBA_HEAD_DIM = 64
GDN_HEAD_DIM = 128
GDN_CHUNK = 64
GDN_PREP_GROUP = 8
CONV_W = 4
VMEM_LIMIT = 56 * 1024 * 1024


def _tile(n, preferred):
    t = min(preferred, n)
    while n % t:
        t //= 2
    return t


def _softplus(x):
    return jnp.maximum(x, 0.0) + jnp.log1p(jnp.exp(-jnp.abs(x)))


def _sigmoid(x):
    return 0.5 + 0.5 * jnp.tanh(0.5 * x)


def _silu(x):
    h = 0.5 * x
    return h + h * jnp.tanh(h)


def _rms(x, gain):
    ms = jnp.mean(x * x, axis=-1, keepdims=True)
    return x * lax.rsqrt(ms + NORM_EPS) * gain


def _dot(a, b):
    return jnp.dot(a, b, preferred_element_type=F32)


def _dot_nt(a, b):
    return lax.dot_general(a, b, (((1,), (1,)), ((), ())), preferred_element_type=F32)


def _split_hi_lo(x):
    hi = x.astype(BF16)
    lo = (x - hi.astype(F32)).astype(BF16)
    return hi, lo


def _norm_matmul_kernel(x_ref, g_ref, w_ref, wab_ref, alog_ref, dtb_ref, y_ref, gb_ref, u_ref, *, n_gdn):
    @pl.when(pl.program_id(1) == 0)
    def _():
        u = _rms(x_ref[...], g_ref[...]).astype(BF16)
        u_ref[...] = u
        ab = _dot(u, wab_ref[...])
        lane = lax.broadcasted_iota(jnp.int32, ab.shape, 1)
        g = -jnp.exp(alog_ref[...]) * _softplus(ab + dtb_ref[...])
        gb_ref[...] = jnp.where(lane < n_gdn, g, _sigmoid(ab))

    y_ref[...] = _dot(u_ref[...], w_ref[...])


def _norm_matmul(x, gain, w, w_ab, alog_pad, dtb_pad, *, n_gdn, tm, tn):
    m, d = x.shape
    n = w.shape[1]
    assert m % tm == 0 and n % tn == 0
    return pl.pallas_call(
        functools.partial(_norm_matmul_kernel, n_gdn=n_gdn),
        grid=(m // tm, n // tn),
        in_specs=[
            pl.BlockSpec((tm, d), lambda i, j: (i, 0)),
            pl.BlockSpec((1, d), lambda i, j: (0, 0)),
            pl.BlockSpec((d, tn), lambda i, j: (0, j)),
            pl.BlockSpec((d, LANES), lambda i, j: (0, 0)),
            pl.BlockSpec((1, LANES), lambda i, j: (0, 0)),
            pl.BlockSpec((1, LANES), lambda i, j: (0, 0)),
        ],
        out_specs=[
            pl.BlockSpec((tm, tn), lambda i, j: (i, j)),
            pl.BlockSpec((tm, LANES), lambda i, j: (i, 0)),
        ],
        out_shape=[jax.ShapeDtypeStruct((m, n), F32), jax.ShapeDtypeStruct((m, LANES), F32)],
        scratch_shapes=[pltpu.VMEM((tm, d), BF16)],
        compiler_params=pltpu.CompilerParams(
            dimension_semantics=("parallel", "arbitrary"), vmem_limit_bytes=VMEM_LIMIT),
        name="norm_inproj",
    )(x, gain, w, w_ab, alog_pad, dtb_pad)


def _softplus2(y):
    neg_abs = lax.bitcast_convert_type(lax.bitcast_convert_type(y, jnp.uint32) | jnp.uint32(0x80000000), F32)
    return jnp.maximum(y, 0.0) + jnp.log2(1.0 + jnp.exp2(neg_abs))


def _cumsum_total_matrix(n):
    j = lax.broadcasted_iota(jnp.int32, (2 * n, 2 * n), 0)
    j = jnp.where(j >= n, j - n, j)
    s = lax.broadcasted_iota(jnp.int32, (2 * n, 2 * n), 1)
    return ((j >= s) | (s >= n)).astype(BF16)


def _sba_prompt_kernel(bias_ref, q_ref, k_ref, v_ref, gate_ref, o_ref, ko_ref, vo_ref,
                       kb_ref, vb_ref, y_s, w_s, carry_s, acc_s, *, tq, tk, scale):
    hd = SBA_HEAD_DIM
    nh = q_ref.shape[-1] // hd
    r = tq // tk
    g = pl.program_id(1)
    i = pl.program_id(2)

    @pl.when(i == 0)
    def _():
        ko_ref[...] = k_ref[...]
        vo_ref[...] = v_ref[...]
        for h in range(nh):
            kb_ref[h] = k_ref[0, :, h * hd:(h + 1) * hd].astype(BF16)
            vb_ref[h] = v_ref[0, :, h * hd:(h + 1) * hd].astype(BF16)

    q = (q_ref[0] * (scale * LOG2E)).astype(BF16)
    qs = [q[:, h * hd:(h + 1) * hd] for h in range(nh)]
    bias = [bias_ref[g * nh + h] * LOG2E for h in range(nh)]
    cmat = _cumsum_total_matrix(tk)
    row = lax.broadcasted_iota(jnp.int32, (tq, tk), 0)
    col = lax.broadcasted_iota(jnp.int32, (tq, tk), 1)
    q0 = i * tq
    heads = range(nh)
    tiles = [(n, h) for n in range(r) for h in heads]
    tid = {(n, h): n * nh + h for n, h in tiles}

    def starts(grp):
        return [pl.multiple_of(((grp + 1) * r - 1 - n) * tk, tk) for n in range(r)]

    def qk_tile(st, slot, t_):
        n, h = t_
        y_s[slot, tid[t_]] = _dot_nt(qs[h], kb_ref[h, pl.ds(st[n], tk), :]) + bias[h]

    def pv_tile(st, t_):
        n, h = t_
        return _dot(w_s[tid[t_]], vb_ref[h, pl.ds(st[n], tk), :])

    def cumsum_tile(y, keep):
        sp = _softplus2(y)
        if keep is not None:
            sp = jnp.where(keep, sp, 0.0)
        return _dot(jnp.concatenate(_split_hi_lo(sp), axis=1), cmat)

    def accumulate(pv):
        for h in heads:
            total = pv[0, h]
            for n in range(1, r):
                total = total + pv[n, h]
            acc_s[h] = acc_s[h] + total

    def weights(y, ct, keep):
        for n in range(r):
            for h in heads:
                x = jnp.exp2(y[n, h] - ct[n, h][:, :tk] - carry_s[h])
                w_s[tid[n, h]] = (x if keep is None else jnp.where(keep[n], x, 0.0)).astype(BF16)
            for h in heads:
                carry_s[h] = carry_s[h] + ct[n, h][:, tk:]

    def group(grp, slot, prev, masked):
        st = starts(grp)
        st_next = starts(jnp.maximum(grp - 1, 0))
        st_prev = None if prev is None else starts(prev)
        keep = [(st[n] + col) < (q0 + row) for n in range(r)] if masked else None
        y, ct, pv = {}, {}, {}
        for t_ in tiles:
            y[t_] = y_s[slot, tid[t_]]
            ct[t_] = cumsum_tile(y[t_], None if keep is None else keep[t_[0]])
            qk_tile(st_next, 1 - slot, t_)
            if prev is not None:
                pv[t_] = pv_tile(st_prev, t_)
        weights(y, ct, keep)
        if prev is not None:
            accumulate(pv)

    carry_s[...] = jnp.zeros_like(carry_s)
    acc_s[...] = jnp.zeros_like(acc_s)
    st_diag = starts(i)
    for t_ in tiles:
        qk_tile(st_diag, 0, t_)
    group(i, 0, None, True)

    def body(t, _):
        group(i - 1 - t, (t + 1) % 2, i - t, False)
        return 0

    lax.fori_loop(0, i, body, 0)
    st_last = starts(0)
    accumulate({t_: pv_tile(st_last, t_) for t_ in tiles})
    o = jnp.concatenate([acc_s[h] for h in heads], axis=1)
    o_ref[0] = (o * _silu(gate_ref[0])).astype(o_ref.dtype)


def _sba_prompt(y3, bias, *, n_heads, tq=256, tk=128, width=256):
    b, t, _ = y3.shape
    width = _tile(n_heads * SBA_HEAD_DIM, width)
    ng = n_heads * SBA_HEAD_DIM // width
    nh = width // SBA_HEAD_DIM
    tq = _tile(t, tq)
    n_tiles = (tq // tk) * nh
    kern = functools.partial(_sba_prompt_kernel, tq=tq, tk=tk, scale=SBA_HEAD_DIM ** -0.5)
    return pl.pallas_call(
        kern,
        grid=(b, ng, t // tq),
        in_specs=[
            pl.BlockSpec(memory_space=pltpu.SMEM),
            pl.BlockSpec((1, tq, width), lambda bi, h, i: (bi, i, h)),
            pl.BlockSpec((1, t, width), lambda bi, h, i: (bi, 0, ng + h)),
            pl.BlockSpec((1, t, width), lambda bi, h, i: (bi, 0, 2 * ng + h)),
            pl.BlockSpec((1, tq, width), lambda bi, h, i: (bi, i, 3 * ng + h)),
        ],
        out_specs=[pl.BlockSpec((1, tq, width), lambda bi, h, i: (bi, i, h)),
                   pl.BlockSpec((1, t, width), lambda bi, h, i: (bi, 0, h)),
                   pl.BlockSpec((1, t, width), lambda bi, h, i: (bi, 0, h))],
        out_shape=[jax.ShapeDtypeStruct((b, t, n_heads * SBA_HEAD_DIM), BF16),
                   jax.ShapeDtypeStruct((b, t, n_heads * SBA_HEAD_DIM), F32),
                   jax.ShapeDtypeStruct((b, t, n_heads * SBA_HEAD_DIM), F32)],
        scratch_shapes=[pltpu.VMEM((nh, t, SBA_HEAD_DIM), BF16), pltpu.VMEM((nh, t, SBA_HEAD_DIM), BF16),
                        pltpu.VMEM((2, n_tiles, tq, tk), F32), pltpu.VMEM((n_tiles, tq, tk), BF16),
                        pltpu.VMEM((nh, tq, tk), F32), pltpu.VMEM((nh, tq, SBA_HEAD_DIM), F32)],
        compiler_params=pltpu.CompilerParams(
            dimension_semantics=("parallel", "parallel", "arbitrary"), vmem_limit_bytes=VMEM_LIMIT),
        name="sba_prompt",
    )(bias, y3, y3, y3, y3)


def _sba_sample_kernel(pt_ref, q_ref, kn_ref, vn_ref, gate_ref, bias_ref, *rest, pages_per_step, past, scale):
    del pt_ref
    pp = pages_per_step
    k_refs = rest[:pp]
    v_refs = rest[pp:2 * pp]
    o_ref, oacc_ref, carry_ref, qbd_ref = rest[2 * pp:]
    hd = SBA_HEAD_DIM
    nh, width = qbd_ref.shape
    page = k_refs[0].shape[1]
    j = pl.program_id(1)
    head_of_lane = lax.broadcasted_iota(jnp.int32, (nh, width), 1) // hd
    own = head_of_lane == lax.broadcasted_iota(jnp.int32, (nh, width), 0)

    @pl.when(j == 0)
    def _():
        qrow = q_ref[0] * scale
        qbd_ref[...] = jnp.where(own, jnp.broadcast_to(qrow, (nh, width)), 0.0).astype(BF16)
        oacc_ref[...] = jnp.zeros_like(oacc_ref)
        carry_ref[...] = jnp.zeros_like(carry_ref)

    cmat = _cumsum_total_matrix(page)
    qbd = qbd_ref[...]
    bias = bias_ref[...]
    carry = carry_ref[...]
    slots = range(pp)
    z = [_dot(qbd, k_refs[s][...].astype(BF16)) + bias for s in slots]
    sp = [_softplus(z[s]) for s in slots]
    ct = [_dot(jnp.concatenate(_split_hi_lo(sp[s]), axis=1), cmat) for s in slots]
    w = []
    for s in slots:
        w.append(jnp.exp(z[s] - ct[s][:, :page] - carry).astype(BF16))
        carry = carry + ct[s][:, page:]
    pv = [_dot_nt(w[s], v_refs[s][...].astype(BF16)) for s in slots]
    oacc = oacc_ref[...]
    for s in slots:
        oacc = oacc + pv[s]
    oacc_ref[...] = oacc
    carry_ref[...] = carry

    @pl.when(j == pl.num_programs(1) - 1)
    def _():
        o = jnp.sum(jnp.where(own, oacc, 0.0), axis=0, keepdims=True)
        kn = kn_ref[0].astype(BF16).astype(F32)
        z_new = jnp.sum(qbd.astype(F32) * kn, axis=-1, keepdims=True) + bias
        q_pos = jnp.full(z_new.shape, past, jnp.int32)
        k_pos = jnp.full(z_new.shape, past, jnp.int32)
        w_new = jnp.where(k_pos < q_pos, jnp.exp(z_new - _softplus(z_new)), 0.0)
        w_new = w_new.astype(BF16).astype(F32)
        w_lane = jnp.sum(jnp.where(own, jnp.broadcast_to(w_new, (nh, width)), 0.0), axis=0, keepdims=True)
        o = o + w_lane * vn_ref[0].astype(BF16).astype(F32)
        o_ref[0] = (o * _silu(gate_ref[0])).astype(o_ref.dtype)


def _sba_sample(ys3, bias_col, cache_k, cache_v, page_table, layer, *, n_heads, pages_per_step=8):
    db = ys3.shape[0]
    width = n_heads * SBA_HEAD_DIM
    n_pages = page_table.shape[1]
    page = cache_k.shape[3]
    pp = pages_per_step
    assert n_pages % pp == 0
    kern = functools.partial(_sba_sample_kernel, pages_per_step=pp, past=n_pages * page,
                             scale=SBA_HEAD_DIM ** -0.5)

    def row_spec(col):
        return pl.BlockSpec((1, 1, width), lambda b, j, pt: (b, 0, col))

    def page_spec(s):
        return pl.BlockSpec((None, None, width, page),
                            lambda b, j, pt: (layer, pt[b, n_pages - 1 - (j * pp + s)], 0, 0))

    grid_spec = pltpu.PrefetchScalarGridSpec(
        num_scalar_prefetch=1,
        grid=(db, n_pages // pp),
        in_specs=[row_spec(0), row_spec(1), row_spec(2), row_spec(3),
                  pl.BlockSpec((n_heads, 1), lambda b, j, pt: (0, 0))]
        + [page_spec(s) for s in range(pp)] + [page_spec(s) for s in range(pp)],
        out_specs=pl.BlockSpec((1, 1, width), lambda b, j, pt: (b, 0, 0)),
        scratch_shapes=[pltpu.VMEM((n_heads, width), F32), pltpu.VMEM((n_heads, page), F32),
                        pltpu.VMEM((n_heads, width), BF16)],
    )
    return pl.pallas_call(
        kern,
        grid_spec=grid_spec,
        out_shape=jax.ShapeDtypeStruct((db, 1, width), BF16),
        compiler_params=pltpu.CompilerParams(
            dimension_semantics=("parallel", "arbitrary"), vmem_limit_bytes=VMEM_LIMIT),
        name="sba_sample",
    )(page_table, ys3, ys3, ys3, ys3, bias_col, *([cache_k] * pp), *([cache_v] * pp))


def _bdot_nt(a, b):
    return jnp.einsum('gmk,gnk->gmn', a, b, preferred_element_type=F32)


def _bdot_split(a, b):
    a_hi, a_lo = _split_hi_lo(a)
    b_hi, b_lo = _split_hi_lo(b)
    lhs = jnp.concatenate([a_hi, a_lo, a_hi], axis=2)
    rhs = jnp.concatenate([b_hi, b_hi, b_lo], axis=1)
    return jnp.einsum('gmk,gkn->gmn', lhs, rhs, preferred_element_type=F32)


def _unit_lower_inverse(a):
    n = a.shape[-1]
    eye = (lax.broadcasted_iota(jnp.int32, (n, n), 0) == lax.broadcasted_iota(jnp.int32, (n, n), 1)).astype(F32)
    x = -a
    p = eye + x
    span = 2
    while span < n:
        x = _bdot_split(x, x)
        p = p + _bdot_split(p, x)
        span *= 2
    return p


def _gdn_prompt_kernel(xq_ref, xk_ref, xv_ref, zg_ref, gb_ref, cwq_ref, cwk_ref, cwv_ref, gn_ref, y_ref, s_ref,
                       k_s, q_s, kb_s, vb_s, g_s, gq_s, n_s, op_s, eg_s, o_s, *, n_heads):
    c = GDN_CHUNK
    dh = GDN_HEAD_DIM
    h = pl.program_id(1)
    t = xq_ref.shape[1]
    nc = t // c
    group = _tile(nc, GDN_PREP_GROUP)
    rows = lax.broadcasted_iota(jnp.int32, (t, dh), 0)
    lane = lax.broadcasted_iota(jnp.int32, (t, dh), 1)

    def conv_silu(x_ref, cw_ref):
        x = x_ref[0]
        cw = cw_ref[...]
        y = None
        for tap in range(CONV_W):
            shift = CONV_W - 1 - tap
            xs = x if shift == 0 else jnp.where(rows >= shift, pltpu.roll(x, shift, axis=0), 0.0)
            term = xs * cw[tap:tap + 1, :]
            y = term if y is None else y + term
        return _silu(y)

    def l2n(x):
        return x * lax.rsqrt(jnp.sum(x * x, axis=-1, keepdims=True) + NORM_EPS)

    q = l2n(conv_silu(xq_ref, cwq_ref)) * (dh ** -0.5)
    k = l2n(conv_silu(xk_ref, cwk_ref))
    v = conv_silu(xv_ref, cwv_ref)

    gb = gb_ref[0]
    g = jnp.sum(jnp.where(lane == h, gb, 0.0), axis=-1, keepdims=True)
    beta = jnp.sum(jnp.where(lane == n_heads + h, gb, 0.0), axis=-1, keepdims=True)

    k_s[...] = k.reshape(nc, c, dh)
    q_s[...] = q.reshape(nc, c, dh)
    kb_s[...] = (k * beta).reshape(nc, c, dh)
    vb_s[...] = (v * beta).reshape(nc, c, dh)
    g_s[...] = jnp.broadcast_to(g, (t, dh)).reshape(nc, c, dh)

    ri = lax.broadcasted_iota(jnp.int32, (c, c), 0)
    ci = lax.broadcasted_iota(jnp.int32, (c, c), 1)
    causal = ri >= ci
    strict = ri > ci
    diag = ri == ci
    tril3 = jnp.broadcast_to(jnp.concatenate([causal.astype(BF16)] * 3, axis=1), (group, c, 3 * c))

    def prep(n, _):
        sl = pl.ds(pl.multiple_of(n * group, group), group)
        k_c = k_s[sl]
        q_c = q_s[sl]
        kb_c = kb_s[sl]
        g_c = g_s[sl]
        g_hi = g_c.astype(BF16)
        g_rest = g_c - g_hi.astype(F32)
        g_mid, g_lo = _split_hi_lo(g_rest)
        cg_c = jnp.einsum('gmk,gkn->gmn', tril3, jnp.concatenate([g_hi, g_mid, g_lo], axis=1),
                          preferred_element_type=F32)
        cg_col = cg_c[:, :, :c]
        cg_row = jnp.sum(jnp.where(diag, cg_col, 0.0), axis=1, keepdims=True)
        decay = jnp.where(causal, jnp.exp(cg_col - cg_row), 0.0)
        k_b = k_c.astype(BF16)
        a_mat = jnp.where(strict, _bdot_nt(kb_c.astype(BF16), k_b) * decay, 0.0)
        attn = jnp.where(causal, _bdot_nt(q_c.astype(BF16), k_b) * decay, 0.0)
        tinv = _unit_lower_inverse(a_mat)
        ecg = jnp.exp(cg_c)
        g_last = cg_c[:, c - 1:c, :]
        uw = _bdot_split(tinv, jnp.concatenate([vb_s[sl], kb_c * ecg], axis=2))
        u_hi, u_lo = _split_hi_lo(uw[:, :, :dh])
        rhs = jnp.concatenate([uw[:, :, dh:].astype(BF16), u_hi, u_lo], axis=2)
        kd_t = jnp.swapaxes(k_c * jnp.exp(g_last - cg_c), 1, 2).astype(BF16)
        kr = jnp.einsum('gkc,gcn->gkn', kd_t, rhs, preferred_element_type=F32)
        ar = jnp.einsum('gic,gcn->gin', attn.astype(BF16), rhs, preferred_element_type=F32)
        gq = jnp.concatenate([kr[:, :, :dh], q_c * ecg - ar[:, :, :dh]], axis=1)
        gq_s[sl] = jnp.concatenate(_split_hi_lo(gq), axis=2)
        n_s[sl] = kr[:, :, dh:2 * dh] + kr[:, :, 2 * dh:]
        op_s[sl] = ar[:, :, dh:2 * dh] + ar[:, :, 2 * dh:]
        eg_s[sl] = jnp.exp(g_last)
        return 0

    lax.fori_loop(0, nc // group, prep, 0)

    def scan(n, s):
        sb = s.astype(BF16)
        gs_qs = _dot(gq_s[n], jnp.concatenate([sb, sb], axis=0))
        o_s[n] = gs_qs[dh:] + op_s[n]
        return s * eg_s[n] + n_s[n] - gs_qs[:dh]

    s_fin = lax.fori_loop(0, nc, scan, jnp.zeros((dh, dh), F32))
    s_ref[0, 0] = s_fin
    o = o_s[...].reshape(t, dh)
    y_ref[0] = (_rms(o, gn_ref[...]) * _silu(zg_ref[0])).astype(y_ref.dtype)


def _gdn_prompt(y3, gb3, conv_w, gdn_norm, *, n_heads, col0):
    b, t, _ = y3.shape
    dh = GDN_HEAD_DIM
    c = GDN_CHUNK
    nc = t // c
    kern = functools.partial(_gdn_prompt_kernel, n_heads=n_heads)

    def seq_spec(off):
        return pl.BlockSpec((1, t, dh), lambda bi, h: (bi, 0, off + h))

    def cw_spec(off):
        return pl.BlockSpec((CONV_W, dh), lambda bi, h: (0, off + h))

    small = pl.BlockSpec((1, LANES), lambda bi, h: (0, 0))
    chunk_f32 = pltpu.VMEM((nc, c, dh), F32)
    return pl.pallas_call(
        kern,
        grid=(b, n_heads),
        in_specs=[seq_spec(col0), seq_spec(col0 + n_heads), seq_spec(col0 + 2 * n_heads),
                  seq_spec(col0 + 3 * n_heads),
                  pl.BlockSpec((1, t, LANES), lambda bi, h: (bi, 0, 0)),
                  cw_spec(0), cw_spec(n_heads), cw_spec(2 * n_heads), small],
        out_specs=[pl.BlockSpec((1, t, dh), lambda bi, h: (bi, 0, h)),
                   pl.BlockSpec((1, 1, dh, dh), lambda bi, h: (bi, h, 0, 0))],
        out_shape=[jax.ShapeDtypeStruct((b, t, n_heads * dh), BF16),
                   jax.ShapeDtypeStruct((b, n_heads, dh, dh), F32)],
        scratch_shapes=[chunk_f32, chunk_f32, chunk_f32, chunk_f32, chunk_f32,
                        pltpu.VMEM((nc, dh + c, 2 * dh), BF16),
                        pltpu.VMEM((nc, dh, dh), F32), chunk_f32,
                        pltpu.VMEM((nc, 1, dh), F32), chunk_f32],
        compiler_params=pltpu.CompilerParams(
            dimension_semantics=("parallel", "parallel"), vmem_limit_bytes=VMEM_LIMIT),
        name="gdn_prompt",
    )(y3, y3, y3, y3, gb3, conv_w, conv_w, conv_w, gdn_norm)


def _gdn_sample_kernel(y_ref, gb_ref, cs_ref, st_ref, cw_ref, gn_ref, yo_ref, cn_ref, so_ref, *, n_heads, col0):
    dh = GDN_HEAD_DIM
    kw = n_heads * dh
    row = y_ref[0]
    xg = row[:, col0:col0 + 3 * kw]
    zg = row[:, col0 + 3 * kw:col0 + 4 * kw]
    buf = cs_ref[0]
    cw = cw_ref[...]
    yc = buf[0:1] * cw[0:1]
    for tap in range(1, CONV_W - 1):
        yc = yc + buf[tap:tap + 1] * cw[tap:tap + 1]
    yc = yc + xg * cw[CONV_W - 1:CONV_W]
    act = _silu(yc)
    for tap in range(CONV_W - 2):
        cn_ref[0, tap:tap + 1, :] = buf[tap + 1:tap + 2]
    cn_ref[0, CONV_W - 2:CONV_W - 1, :] = xg

    g_all = gb_ref[0]
    beta_all = g_all
    eye =lax.broadcasted_iota(jnp.int32, (dh, dh), 0) == lax.broadcasted_iota(jnp.int32, (dh, dh), 1)

    def to_col(r):
        return jnp.sum(jnp.where(eye, jnp.broadcast_to(r, (dh, dh)), 0.0), axis=1, keepdims=True)

    def l2n(x):
        return x * lax.rsqrt(jnp.sum(x * x, axis=-1, keepdims=True) + NORM_EPS)

    gn = gn_ref[...]
    for h in range(n_heads):
        qh = l2n(act[:, h * dh:(h + 1) * dh]) * (dh ** -0.5)
        kh = l2n(act[:, kw + h * dh:kw + (h + 1) * dh])
        vh = act[:, 2 * kw + h * dh:2 * kw + (h + 1) * dh]
        g_h = g_all[:, h:h + 1]
        beta_h = beta_all[:, n_heads + h:n_heads + h + 1]
        s = st_ref[0, h] * jnp.exp(g_h)
        kcol = to_col(kh)
        ks = jnp.sum(kcol * s, axis=0, keepdims=True)
        delta = beta_h * (vh - ks)
        s = s + kcol * delta
        so_ref[0, h] = s
        o = jnp.sum(to_col(qh) * s, axis=0, keepdims=True)
        yo_ref[0, :, h * dh:(h + 1) * dh] = (
            _rms(o, gn) * _silu(zg[:, h * dh:(h + 1) * dh])).astype(yo_ref.dtype)


def _gdn_sample(ys3, gb3, conv_state, gdn_state, conv_w, gdn_norm, *, n_heads, col0):
    db, _, n = ys3.shape
    dh = GDN_HEAD_DIM
    kw = n_heads * dh
    kern = functools.partial(_gdn_sample_kernel, n_heads=n_heads, col0=col0)
    small = pl.BlockSpec((1, LANES), lambda b: (0, 0))
    return pl.pallas_call(
        kern,
        grid=(db,),
        in_specs=[pl.BlockSpec((1, 1, n), lambda b: (b, 0, 0)),
                  pl.BlockSpec((1, 1, LANES), lambda b: (b, 0, 0)),
                  pl.BlockSpec((1, CONV_W - 1, 3 * kw), lambda b: (b, 0, 0)),
                  pl.BlockSpec((1, n_heads, dh, dh), lambda b: (b, 0, 0, 0)),
                  pl.BlockSpec((CONV_W, 3 * kw), lambda b: (0, 0)),
                  small],
        out_specs=[pl.BlockSpec((1, 1, kw), lambda b: (b, 0, 0)),
                   pl.BlockSpec((1, CONV_W - 1, 3 * kw), lambda b: (b, 0, 0)),
                   pl.BlockSpec((1, n_heads, dh, dh), lambda b: (b, 0, 0, 0))],
        out_shape=[jax.ShapeDtypeStruct((db, 1, kw), BF16),
                   jax.ShapeDtypeStruct((db, CONV_W - 1, 3 * kw), F32),
                   jax.ShapeDtypeStruct((db, n_heads, dh, dh), F32)],
        compiler_params=pltpu.CompilerParams(
            dimension_semantics=("parallel",), vmem_limit_bytes=VMEM_LIMIT),
        name="gdn_sample",
    )(ys3, gb3, conv_state, gdn_state, conv_w, gdn_norm)


def _merge_kernel(ys_ref, yg_ref, ms_ref, mg_ref, x_ref, wus_ref, wug_ref, wo_ref, h_ref):
    merged = (_sigmoid(ms_ref[...]) * _dot(ys_ref[...], wus_ref[...])
              + _sigmoid(mg_ref[...]) * _dot(yg_ref[...], wug_ref[...]))
    h_ref[...] = x_ref[...] + _dot(merged.astype(BF16), wo_ref[...])


def _merge(ys, yg, y, x, w_up_sba, w_up_gdn, w_o, *, tm, gate_col):
    m, d = x.shape
    kw = ys.shape[1]

    def resident(shape):
        return pl.BlockSpec(shape, lambda i: (0, 0), pipeline_mode=pl.Buffered(1))

    return pl.pallas_call(
        _merge_kernel,
        grid=(m // tm,),
        in_specs=[pl.BlockSpec((tm, kw), lambda i: (i, 0)),
                  pl.BlockSpec((tm, kw), lambda i: (i, 0)),
                  pl.BlockSpec((tm, d), lambda i: (i, gate_col)),
                  pl.BlockSpec((tm, d), lambda i: (i, gate_col + 1)),
                  pl.BlockSpec((tm, d), lambda i: (i, 0)),
                  resident(w_up_sba.shape), resident(w_up_gdn.shape), resident(w_o.shape)],
        out_specs=pl.BlockSpec((tm, d), lambda i: (i, 0)),
        out_shape=jax.ShapeDtypeStruct((m, d), F32),
        compiler_params=pltpu.CompilerParams(
            dimension_semantics=("parallel",), vmem_limit_bytes=VMEM_LIMIT),
        name="merge_out",
    )(ys, yg, y, y, x, w_up_sba, w_up_gdn, w_o)


def _ple_kernel(h_ref, p_ref, np_ref, nf_ref, wg_ref, wp_ref, o_ref, *, final_norm):
    h = h_ref[...]
    gate = _sigmoid(_dot(_rms(h, np_ref[...]).astype(BF16), wg_ref[...]))
    h = h + gate * _dot(p_ref[...].astype(BF16), wp_ref[...])
    o_ref[...] = _rms(h, nf_ref[...]) if final_norm else h


def _ple(h, p, norm_ple, norm_final, w_gate, w_proj, *, tm, final_norm):
    m, d = h.shape
    pd = p.shape[1]

    def resident(shape):
        return pl.BlockSpec(shape, lambda i: (0, 0), pipeline_mode=pl.Buffered(1))

    return pl.pallas_call(
        functools.partial(_ple_kernel, final_norm=final_norm),
        grid=(m // tm,),
        in_specs=[pl.BlockSpec((tm, d), lambda i: (i, 0)),
                  pl.BlockSpec((tm, pd), lambda i: (i, 0)),
                  resident((1, d)), resident((1, d)), resident(w_gate.shape), resident(w_proj.shape)],
        out_specs=pl.BlockSpec((tm, d), lambda i: (i, 0)),
        out_shape=jax.ShapeDtypeStruct((m, d), F32),
        compiler_params=pltpu.CompilerParams(
            dimension_semantics=("parallel",), vmem_limit_bytes=VMEM_LIMIT),
        name="ple_norm",
    )(h, p, norm_ple, norm_final, w_gate, w_proj)


def _pad_row(v):
    return jnp.zeros((1, LANES), F32).at[0, :v.shape[0]].set(v.astype(F32))


def kernel(x_prompt, x_sample, cache_k, cache_v, state_conv, state_gdn, page_table, p_prompt, p_sample, norm_mix, w_in, sba_bias, conv_w, gdn_a_log, gdn_dt_bias, gdn_norm, w_up_sba, w_up_gdn, w_o, norm_ple, w_ple_gate, w_ple_proj, norm_final):
    depth = w_in.shape[0]
    b, t, d = x_prompt.shape
    db, dt, _ = x_sample.shape
    assert dt == 1
    n_sba = sba_bias.shape[1]
    n_gdn = gdn_a_log.shape[1]
    sba_w = n_sba * SBA_HEAD_DIM
    gdn_w = n_gdn * GDN_HEAD_DIM
    conv_ch = 3 * gdn_w
    ab0 = 4 * sba_w + conv_ch
    n_main = w_in.shape[2] - 2 * n_gdn
    gdn_col0 = 4 * sba_w // LANES
    gate_col = (4 * sba_w + conv_ch + gdn_w) // d
    page = cache_k.shape[2]
    ck = jnp.transpose(cache_k, (0, 1, 3, 4, 2)).reshape(cache_k.shape[0], cache_k.shape[1], sba_w, page)
    cv = jnp.transpose(cache_v, (0, 1, 3, 4, 2)).reshape(cache_v.shape[0], cache_v.shape[1], sba_w, page)

    hp = x_prompt.reshape(b * t, d)
    hs = x_sample.reshape(db, d)
    outs = [[] for _ in range(8)]
    for layer in range(depth):
        w_l = w_in[layer]
        w_main = jnp.concatenate([w_l[:, :ab0], w_l[:, ab0 + 2 * n_gdn:]], axis=1).astype(BF16)
        w_ab = jnp.zeros((d, LANES), BF16).at[:, :2 * n_gdn].set(w_l[:, ab0:ab0 + 2 * n_gdn].astype(BF16))
        gain = norm_mix[layer].reshape(1, d)
        wus = w_up_sba[layer].astype(BF16)
        wug = w_up_gdn[layer].astype(BF16)
        wo = w_o[layer].astype(BF16)
        wpg = w_ple_gate[layer].astype(BF16)
        wpp = w_ple_proj[layer].astype(BF16)
        alog_pad = _pad_row(gdn_a_log[layer])
        dtb_pad = _pad_row(gdn_dt_bias[layer])
        gnorm = gdn_norm[layer].reshape(1, GDN_HEAD_DIM)
        nple = norm_ple[layer].reshape(1, d)
        nfin = norm_final.reshape(1, d)
        last = layer == depth - 1

        y_p, gb_p = _norm_matmul(hp, gain, w_main, w_ab, alog_pad, dtb_pad, n_gdn=n_gdn,
                                 tm=_tile(b * t, 1024), tn=_tile(n_main, 512))
        y3 = y_p.reshape(b, t, n_main)
        ys_p, k_p, v_p = _sba_prompt(y3, sba_bias[layer].astype(F32), n_heads=n_sba)
        yg_p, s_p = _gdn_prompt(y3, gb_p.reshape(b, t, LANES), conv_w[layer], gnorm,
                                n_heads=n_gdn, col0=gdn_col0)
        h_p = _merge(ys_p.reshape(b * t, sba_w), yg_p.reshape(b * t, gdn_w), y_p, hp, wus, wug, wo,
                     tm=_tile(b * t, 256), gate_col=gate_col)
        hp = _ple(h_p, p_prompt[layer].reshape(b * t, -1), nple, nfin, wpg, wpp, tm=_tile(b * t, 512),
                  final_norm=last)
        outs[0].append(k_p.reshape(b, t, n_sba, SBA_HEAD_DIM))
        outs[1].append(v_p.reshape(b, t, n_sba, SBA_HEAD_DIM))
        outs[2].append(y3[:, t - (CONV_W - 1):, 4 * sba_w:4 * sba_w + conv_ch])
        outs[3].append(s_p)

        y_s, gb_s = _norm_matmul(hs, gain, w_main, w_ab, alog_pad, dtb_pad, n_gdn=n_gdn,
                                 tm=db, tn=_tile(n_main, 512))
        ys3 = y_s.reshape(db, 1, n_main)
        ys_s = _sba_sample(ys3, sba_bias[layer].astype(F32).reshape(n_sba, 1), ck, cv, page_table, layer,
                           n_heads=n_sba, pages_per_step=_tile(page_table.shape[1], 16))
        yg_s, cn_s, s_s = _gdn_sample(ys3, gb_s.reshape(db, 1, LANES), state_conv[layer], state_gdn[layer],
                                      conv_w[layer], gnorm, n_heads=n_gdn, col0=4 * sba_w)
        h_s = _merge(ys_s.reshape(db, sba_w), yg_s.reshape(db, gdn_w), y_s, hs, wus, wug, wo,
                     tm=db, gate_col=gate_col)
        hs = _ple(h_s, p_sample[layer].reshape(db, -1), nple, nfin, wpg, wpp, tm=db, final_norm=last)
        outs[4].append(ys3[:, :, sba_w:2 * sba_w].reshape(db, 1, n_sba, SBA_HEAD_DIM))
        outs[5].append(ys3[:, :, 2 * sba_w:3 * sba_w].reshape(db, 1, n_sba, SBA_HEAD_DIM))
        outs[6].append(cn_s)
        outs[7].append(s_s)

    return (hp.reshape(b, t, d), hs.reshape(db, 1, d)) + tuple(jnp.stack(o) for o in outs)
```

```python
import functools

import jax
import jax.numpy as jnp
from jax import lax
from jax.experimental import pallas as pl
from jax.experimental.pallas import tpu as pltpu

F32 = jnp.float32
BF16 = jnp.bfloat16
NORM_EPS = 1e-6
LOG2E = 1.4426950408889634
LANES = 128
SBA_HEAD_DIM = 64
GDN_HEAD_DIM = 128
GDN_CHUNK = 64
GDN_PREP_GROUP = 16
CONV_W = 4
VMEM_LIMIT = 56 * 1024 * 1024


def _tile(n, preferred):
    t = min(preferred, n)
    while n % t:
        t //= 2
    return t


def _softplus(x):
    return jnp.maximum(x, 0.0) + jnp.log1p(jnp.exp(-jnp.abs(x)))


def _sigmoid(x):
    return 0.5 + 0.5 * jnp.tanh(0.5 * x)


def _silu(x):
    h = 0.5 * x
    return h + h * jnp.tanh(h)


def _rms(x, gain):
    ms = jnp.mean(x * x, axis=-1, keepdims=True)
    return x * lax.rsqrt(ms + NORM_EPS) * gain


def _dot(a, b):
    return jnp.dot(a, b, preferred_element_type=F32)


def _dot_nt(a, b):
    return lax.dot_general(a, b, (((1,), (1,)), ((), ())), preferred_element_type=F32)


def _split_hi_lo(x):
    hi = x.astype(BF16)
    lo = (x - hi.astype(F32)).astype(BF16)
    return hi, lo


def _norm_matmul_kernel(x_ref, g_ref, w_ref, wab_ref, alog_ref, dtb_ref, y_ref, gb_ref, u_ref, *, n_gdn):
    @pl.when(pl.program_id(1) == 0)
    def _():
        u = _rms(x_ref[...], g_ref[...]).astype(BF16)
        u_ref[...] = u
        ab = _dot(u, wab_ref[...])
        lane = lax.broadcasted_iota(jnp.int32, ab.shape, 1)
        g = -jnp.exp(alog_ref[...]) * _softplus(ab + dtb_ref[...])
        gb_ref[...] = jnp.where(lane < n_gdn, g, _sigmoid(ab))

    y_ref[...] = _dot(u_ref[...], w_ref[...])


def _norm_matmul(x, gain, w, w_ab, alog_pad, dtb_pad, *, n_gdn, tm, tn):
    m, d = x.shape
    n = w.shape[1]
    assert m % tm == 0 and n % tn == 0
    return pl.pallas_call(
        functools.partial(_norm_matmul_kernel, n_gdn=n_gdn),
        grid=(m // tm, n // tn),
        in_specs=[
            pl.BlockSpec((tm, d), lambda i, j: (i, 0)),
            pl.BlockSpec((1, d), lambda i, j: (0, 0)),
            pl.BlockSpec((d, tn), lambda i, j: (0, j)),
            pl.BlockSpec((d, LANES), lambda i, j: (0, 0)),
            pl.BlockSpec((1, LANES), lambda i, j: (0, 0)),
            pl.BlockSpec((1, LANES), lambda i, j: (0, 0)),
        ],
        out_specs=[
            pl.BlockSpec((tm, tn), lambda i, j: (i, j)),
            pl.BlockSpec((tm, LANES), lambda i, j: (i, 0)),
        ],
        out_shape=[jax.ShapeDtypeStruct((m, n), F32), jax.ShapeDtypeStruct((m, LANES), F32)],
        scratch_shapes=[pltpu.VMEM((tm, d), BF16)],
        compiler_params=pltpu.CompilerParams(
            dimension_semantics=("parallel", "arbitrary"), vmem_limit_bytes=VMEM_LIMIT),
        name="norm_inproj",
    )(x, gain, w, w_ab, alog_pad, dtb_pad)


def _softplus2(y):
    neg_abs = lax.bitcast_convert_type(lax.bitcast_convert_type(y, jnp.uint32) | jnp.uint32(0x80000000), F32)
    return jnp.maximum(y, 0.0) + jnp.log2(1.0 + jnp.exp2(neg_abs))


def _cumsum_total_matrix(n):
    j = lax.broadcasted_iota(jnp.int32, (2 * n, 2 * n), 0)
    j = jnp.where(j >= n, j - n, j)
    s = lax.broadcasted_iota(jnp.int32, (2 * n, 2 * n), 1)
    return ((j >= s) | (s >= n)).astype(BF16)


def _sba_prompt_kernel(bias_ref, q_ref, k_ref, v_ref, gate_ref, o_ref, ko_ref, vo_ref,
                       kb_ref, vb_ref, y_s, w_s, carry_s, acc_s, *, tq, tk, scale):
    hd = SBA_HEAD_DIM
    nh = q_ref.shape[-1] // hd
    r = tq // tk
    g = pl.program_id(1)
    i = pl.program_id(2)

    @pl.when(i == 0)
    def _():
        ko_ref[...] = k_ref[...]
        vo_ref[...] = v_ref[...]
        for h in range(nh):
            kb_ref[h] = k_ref[0, :, h * hd:(h + 1) * hd].astype(BF16)
            vb_ref[h] = v_ref[0, :, h * hd:(h + 1) * hd].astype(BF16)

    q = (q_ref[0] * (scale * LOG2E)).astype(BF16)
    qs = [q[:, h * hd:(h + 1) * hd] for h in range(nh)]
    bias = [bias_ref[g * nh + h] * LOG2E for h in range(nh)]
    cmat = _cumsum_total_matrix(tk)
    row = lax.broadcasted_iota(jnp.int32, (tq, tk), 0)
    col = lax.broadcasted_iota(jnp.int32, (tq, tk), 1)
    q0 = i * tq
    heads = range(nh)
    tiles = [(n, h) for n in range(r) for h in heads]
    tid = {(n, h): n * nh + h for n, h in tiles}

    def starts(grp):
        return [pl.multiple_of(((grp + 1) * r - 1 - n) * tk, tk) for n in range(r)]

    def qk_tile(st, slot, t_):
        n, h = t_
        y_s[slot, tid[t_]] = _dot_nt(qs[h], kb_ref[h, pl.ds(st[n], tk), :]) + bias[h]

    def pv_tile(st, t_):
        n, h = t_
        return _dot(w_s[tid[t_]], vb_ref[h, pl.ds(st[n], tk), :])

    def cumsum_tile(y, keep):
        sp = _softplus2(y)
        if keep is not None:
            sp = jnp.where(keep, sp, 0.0)
        return _dot(jnp.concatenate(_split_hi_lo(sp), axis=1), cmat)

    def accumulate(pv):
        for h in heads:
            total = pv[0, h]
            for n in range(1, r):
                total = total + pv[n, h]
            acc_s[h] = acc_s[h] + total

    def weights(y, ct, keep):
        for n in range(r):
            for h in heads:
                x = jnp.exp2(y[n, h] - ct[n, h][:, :tk] - carry_s[h])
                w_s[tid[n, h]] = (x if keep is None else jnp.where(keep[n], x, 0.0)).astype(BF16)
            for h in heads:
                carry_s[h] = carry_s[h] + ct[n, h][:, tk:]

    def group(grp, slot, prev, masked):
        st = starts(grp)
        st_next = starts(jnp.maximum(grp - 1, 0))
        st_prev = None if prev is None else starts(prev)
        keep = [(st[n] + col) < (q0 + row) for n in range(r)] if masked else None
        y, ct, pv = {}, {}, {}
        for t_ in tiles:
            y[t_] = y_s[slot, tid[t_]]
            ct[t_] = cumsum_tile(y[t_], None if keep is None else keep[t_[0]])
            qk_tile(st_next, 1 - slot, t_)
            if prev is not None:
                pv[t_] = pv_tile(st_prev, t_)
        weights(y, ct, keep)
        if prev is not None:
            accumulate(pv)

    carry_s[...] = jnp.zeros_like(carry_s)
    acc_s[...] = jnp.zeros_like(acc_s)
    st_diag = starts(i)
    for t_ in tiles:
        qk_tile(st_diag, 0, t_)
    group(i, 0, None, True)

    def body(t, _):
        group(i - 1 - t, (t + 1) % 2, i - t, False)
        return 0

    lax.fori_loop(0, i, body, 0)
    st_last = starts(0)
    accumulate({t_: pv_tile(st_last, t_) for t_ in tiles})
    o = jnp.concatenate([acc_s[h] for h in heads], axis=1)
    o_ref[0] = (o * _silu(gate_ref[0])).astype(o_ref.dtype)


def _sba_prompt(y3, bias, *, n_heads, tq=256, tk=128, width=256):
    b, t, _ = y3.shape
    width = _tile(n_heads * SBA_HEAD_DIM, width)
    ng = n_heads * SBA_HEAD_DIM // width
    nh = width // SBA_HEAD_DIM
    tq = _tile(t, tq)
    n_tiles = (tq // tk) * nh
    kern = functools.partial(_sba_prompt_kernel, tq=tq, tk=tk, scale=SBA_HEAD_DIM ** -0.5)
    return pl.pallas_call(
        kern,
        grid=(b, ng, t // tq),
        in_specs=[
            pl.BlockSpec(memory_space=pltpu.SMEM),
            pl.BlockSpec((1, tq, width), lambda bi, h, i: (bi, i, h)),
            pl.BlockSpec((1, t, width), lambda bi, h, i: (bi, 0, ng + h)),
            pl.BlockSpec((1, t, width), lambda bi, h, i: (bi, 0, 2 * ng + h)),
            pl.BlockSpec((1, tq, width), lambda bi, h, i: (bi, i, 3 * ng + h)),
        ],
        out_specs=[pl.BlockSpec((1, tq, width), lambda bi, h, i: (bi, i, h)),
                   pl.BlockSpec((1, t, width), lambda bi, h, i: (bi, 0, h)),
                   pl.BlockSpec((1, t, width), lambda bi, h, i: (bi, 0, h))],
        out_shape=[jax.ShapeDtypeStruct((b, t, n_heads * SBA_HEAD_DIM), BF16),
                   jax.ShapeDtypeStruct((b, t, n_heads * SBA_HEAD_DIM), F32),
                   jax.ShapeDtypeStruct((b, t, n_heads * SBA_HEAD_DIM), F32)],
        scratch_shapes=[pltpu.VMEM((nh, t, SBA_HEAD_DIM), BF16), pltpu.VMEM((nh, t, SBA_HEAD_DIM), BF16),
                        pltpu.VMEM((2, n_tiles, tq, tk), F32), pltpu.VMEM((n_tiles, tq, tk), BF16),
                        pltpu.VMEM((nh, tq, tk), F32), pltpu.VMEM((nh, tq, SBA_HEAD_DIM), F32)],
        compiler_params=pltpu.CompilerParams(
            dimension_semantics=("parallel", "parallel", "arbitrary"), vmem_limit_bytes=VMEM_LIMIT),
        name="sba_prompt",
    )(bias, y3, y3, y3, y3)


def _sba_sample_kernel(pt_ref, q_ref, kn_ref, vn_ref, gate_ref, bias_ref, *rest, pages_per_step, past, scale):
    del pt_ref
    pp = pages_per_step
    k_refs = rest[:pp]
    v_refs = rest[pp:2 * pp]
    o_ref, oacc_ref, carry_ref, qbd_ref = rest[2 * pp:]
    hd = SBA_HEAD_DIM
    nh, width = qbd_ref.shape
    page = k_refs[0].shape[1]
    j = pl.program_id(1)
    head_of_lane = lax.broadcasted_iota(jnp.int32, (nh, width), 1) // hd
    own = head_of_lane == lax.broadcasted_iota(jnp.int32, (nh, width), 0)

    @pl.when(j == 0)
    def _():
        qrow = q_ref[0] * scale
        qbd_ref[...] = jnp.where(own, jnp.broadcast_to(qrow, (nh, width)), 0.0).astype(BF16)
        oacc_ref[...] = jnp.zeros_like(oacc_ref)
        carry_ref[...] = jnp.zeros_like(carry_ref)

    cmat = _cumsum_total_matrix(page)
    qbd = qbd_ref[...]
    bias = bias_ref[...]
    carry = carry_ref[...]
    slots = range(pp)
    z = [_dot(qbd, k_refs[s][...].astype(BF16)) + bias for s in slots]
    sp = [_softplus(z[s]) for s in slots]
    ct = [_dot(jnp.concatenate(_split_hi_lo(sp[s]), axis=1), cmat) for s in slots]
    w = []
    for s in slots:
        w.append(jnp.exp(z[s] - ct[s][:, :page] - carry).astype(BF16))
        carry = carry + ct[s][:, page:]
    pv = [_dot_nt(w[s], v_refs[s][...].astype(BF16)) for s in slots]
    oacc = oacc_ref[...]
    for s in slots:
        oacc = oacc + pv[s]
    oacc_ref[...] = oacc
    carry_ref[...] = carry

    @pl.when(j == pl.num_programs(1) - 1)
    def _():
        o = jnp.sum(jnp.where(own, oacc, 0.0), axis=0, keepdims=True)
        kn = kn_ref[0].astype(BF16).astype(F32)
        z_new = jnp.sum(qbd.astype(F32) * kn, axis=-1, keepdims=True) + bias
        q_pos = jnp.full(z_new.shape, past, jnp.int32)
        k_pos = jnp.full(z_new.shape, past, jnp.int32)
        w_new = jnp.where(k_pos < q_pos, jnp.exp(z_new - _softplus(z_new)), 0.0)
        w_new = w_new.astype(BF16).astype(F32)
        w_lane = jnp.sum(jnp.where(own, jnp.broadcast_to(w_new, (nh, width)), 0.0), axis=0, keepdims=True)
        o = o + w_lane * vn_ref[0].astype(BF16).astype(F32)
        o_ref[0] = (o * _silu(gate_ref[0])).astype(o_ref.dtype)


def _sba_sample(ys3, bias_col, cache_k, cache_v, page_table, layer, *, n_heads, pages_per_step=8):
    db = ys3.shape[0]
    width = n_heads * SBA_HEAD_DIM
    n_pages = page_table.shape[1]
    page = cache_k.shape[3]
    pp = pages_per_step
    assert n_pages % pp == 0
    kern = functools.partial(_sba_sample_kernel, pages_per_step=pp, past=n_pages * page,
                             scale=SBA_HEAD_DIM ** -0.5)

    def row_spec(col):
        return pl.BlockSpec((1, 1, width), lambda b, j, pt: (b, 0, col))

    def page_spec(s):
        return pl.BlockSpec((None, None, width, page),
                            lambda b, j, pt: (layer, pt[b, n_pages - 1 - (j * pp + s)], 0, 0))

    grid_spec = pltpu.PrefetchScalarGridSpec(
        num_scalar_prefetch=1,
        grid=(db, n_pages // pp),
        in_specs=[row_spec(0), row_spec(1), row_spec(2), row_spec(3),
                  pl.BlockSpec((n_heads, 1), lambda b, j, pt: (0, 0))]
        + [page_spec(s) for s in range(pp)] + [page_spec(s) for s in range(pp)],
        out_specs=pl.BlockSpec((1, 1, width), lambda b, j, pt: (b, 0, 0)),
        scratch_shapes=[pltpu.VMEM((n_heads, width), F32), pltpu.VMEM((n_heads, page), F32),
                        pltpu.VMEM((n_heads, width), BF16)],
    )
    return pl.pallas_call(
        kern,
        grid_spec=grid_spec,
        out_shape=jax.ShapeDtypeStruct((db, 1, width), BF16),
        compiler_params=pltpu.CompilerParams(
            dimension_semantics=("parallel", "arbitrary"), vmem_limit_bytes=VMEM_LIMIT),
        name="sba_sample",
    )(page_table, ys3, ys3, ys3, ys3, bias_col, *([cache_k] * pp), *([cache_v] * pp))


def _bdot_nt(a, b):
    return jnp.einsum('gmk,gnk->gmn', a, b, preferred_element_type=F32)


def _bdot_split(a, b):
    a_hi, a_lo = _split_hi_lo(a)
    b_hi, b_lo = _split_hi_lo(b)
    lhs = jnp.concatenate([a_hi, a_lo, a_hi], axis=2)
    rhs = jnp.concatenate([b_hi, b_hi, b_lo], axis=1)
    return jnp.einsum('gmk,gkn->gmn', lhs, rhs, preferred_element_type=F32)


def _unit_lower_inverse(a):
    n = a.shape[-1]
    eye = (lax.broadcasted_iota(jnp.int32, (n, n), 0) == lax.broadcasted_iota(jnp.int32, (n, n), 1)).astype(F32)
    x = -a
    p = eye + x
    span = 2
    while span < n:
        x = _bdot_split(x, x)
        p = p + _bdot_split(p, x)
        span *= 2
    return p


def _gdn_prompt_kernel(xq_ref, xk_ref, xv_ref, zg_ref, gb_ref, cwq_ref, cwk_ref, cwv_ref, gn_ref, y_ref, s_ref,
                       k_s, q_s, kb_s, vb_s, g_s, gq_s, n_s, op_s, eg_s, o_s, *, n_heads):
    c = GDN_CHUNK
    dh = GDN_HEAD_DIM
    hps = xq_ref.shape[2] // dh
    h0 = pl.program_id(1) * hps
    t = xq_ref.shape[1]
    nc = t // c
    group = _tile(nc, GDN_PREP_GROUP)
    rows = lax.broadcasted_iota(jnp.int32, (t, dh), 0)
    lane = lax.broadcasted_iota(jnp.int32, (t, dh), 1)
    ri = lax.broadcasted_iota(jnp.int32, (c, c), 0)
    ci = lax.broadcasted_iota(jnp.int32, (c, c), 1)
    causal = ri >= ci
    strict = ri > ci
    diag = ri == ci
    tril3 = jnp.broadcast_to(jnp.concatenate([causal.astype(BF16)] * 3, axis=1), (group, c, 3 * c))

    def l2n(x):
        return x * lax.rsqrt(jnp.sum(x * x, axis=-1, keepdims=True) + NORM_EPS)

    for hh in range(hps):
        lanes = slice(hh * dh, (hh + 1) * dh)

        def conv_silu(x_ref, cw_ref):
            x = x_ref[0, :, lanes]
            cw = cw_ref[:, lanes]
            y = None
            for tap in range(CONV_W):
                shift = CONV_W - 1 - tap
                xs = x if shift == 0 else jnp.where(rows >= shift, pltpu.roll(x, shift, axis=0), 0.0)
                term = xs * cw[tap:tap + 1, :]
                y = term if y is None else y + term
            return _silu(y)

        q = l2n(conv_silu(xq_ref, cwq_ref)) * (dh ** -0.5)
        k = l2n(conv_silu(xk_ref, cwk_ref))
        v = conv_silu(xv_ref, cwv_ref)

        gb = gb_ref[0]
        g = jnp.sum(jnp.where(lane == h0 + hh, gb, 0.0), axis=-1, keepdims=True)
        beta = jnp.sum(jnp.where(lane == n_heads + h0 + hh, gb, 0.0), axis=-1, keepdims=True)

        k_s[...] = k.reshape(nc, c, dh)
        q_s[...] = q.reshape(nc, c, dh)
        kb_s[...] = (k * beta).reshape(nc, c, dh)
        vb_s[...] = (v * beta).reshape(nc, c, dh)
        g_s[...] = jnp.broadcast_to(g, (t, dh)).reshape(nc, c, dh)

        def prep(n, _):
            sl = pl.ds(pl.multiple_of(n * group, group), group)
            k_c = k_s[sl]
            q_c = q_s[sl]
            kb_c = kb_s[sl]
            g_c = g_s[sl]
            g_hi = g_c.astype(BF16)
            g_rest = g_c - g_hi.astype(F32)
            g_mid, g_lo = _split_hi_lo(g_rest)
            cg_c = jnp.einsum('gmk,gkn->gmn', tril3, jnp.concatenate([g_hi, g_mid, g_lo], axis=1),
                              preferred_element_type=F32)
            cg_col = cg_c[:, :, :c]
            cg_row = jnp.sum(jnp.where(diag, cg_col, 0.0), axis=1, keepdims=True)
            decay = jnp.where(causal, jnp.exp(cg_col - cg_row), 0.0)
            k_b = k_c.astype(BF16)
            a_mat = jnp.where(strict, _bdot_nt(kb_c.astype(BF16), k_b) * decay, 0.0)
            attn = jnp.where(causal, _bdot_nt(q_c.astype(BF16), k_b) * decay, 0.0)
            tinv = _unit_lower_inverse(a_mat)
            ecg = jnp.exp(cg_c)
            g_last = cg_c[:, c - 1:c, :]
            uw = _bdot_split(tinv, jnp.concatenate([vb_s[sl], kb_c * ecg], axis=2))
            u_hi, u_lo = _split_hi_lo(uw[:, :, :dh])
            rhs = jnp.concatenate([uw[:, :, dh:].astype(BF16), u_hi, u_lo], axis=2)
            kd_t = jnp.swapaxes(k_c * jnp.exp(g_last - cg_c), 1, 2).astype(BF16)
            kr = jnp.einsum('gkc,gcn->gkn', kd_t, rhs, preferred_element_type=F32)
            ar = jnp.einsum('gic,gcn->gin', attn.astype(BF16), rhs, preferred_element_type=F32)
            gq = jnp.concatenate([kr[:, :, :dh], q_c * ecg - ar[:, :, :dh]], axis=1)
            gq_s[hh, sl] = jnp.concatenate(_split_hi_lo(gq), axis=2)
            n_s[hh, sl] = kr[:, :, dh:2 * dh] + kr[:, :, 2 * dh:]
            op_s[hh, sl] = ar[:, :, dh:2 * dh] + ar[:, :, 2 * dh:]
            eg_s[hh, sl] = jnp.exp(g_last)
            return 0

        lax.fori_loop(0, nc // group, prep, 0)

    def scan(n, states):
        sb = [s.astype(BF16) for s in states]
        gs_qs = [_dot(gq_s[hh, n], jnp.concatenate([sb[hh], sb[hh]], axis=0)) for hh in range(hps)]
        for hh in range(hps):
            o_s[hh, n] = gs_qs[hh][dh:] + op_s[hh, n]
        return tuple(states[hh] * eg_s[hh, n] + n_s[hh, n] - gs_qs[hh][:dh] for hh in range(hps))

    s_fin = lax.fori_loop(0, nc, scan, tuple(jnp.zeros((dh, dh), F32) for _ in range(hps)))
    for hh in range(hps):
        lanes = slice(hh * dh, (hh + 1) * dh)
        s_ref[0, hh] = s_fin[hh]
        o = o_s[hh].reshape(t, dh)
        y_ref[0, :, lanes] = (_rms(o, gn_ref[...]) * _silu(zg_ref[0, :, lanes])).astype(y_ref.dtype)


def _gdn_prompt(y3, gb3, conv_w, gdn_norm, *, n_heads, col0, heads_per_step=2):
    b, t, _ = y3.shape
    dh = GDN_HEAD_DIM
    c = GDN_CHUNK
    nc = t // c
    hps = heads_per_step
    assert n_heads % hps == 0 and col0 % hps == 0
    wide = hps * dh
    kern = functools.partial(_gdn_prompt_kernel, n_heads=n_heads)

    def seq_spec(off):
        return pl.BlockSpec((1, t, wide), lambda bi, h: (bi, 0, off // hps + h))

    def cw_spec(off):
        return pl.BlockSpec((CONV_W, wide), lambda bi, h: (0, off // hps + h))

    small = pl.BlockSpec((1, LANES), lambda bi, h: (0, 0))
    chunk_f32 = pltpu.VMEM((nc, c, dh), F32)
    return pl.pallas_call(
        kern,
        grid=(b, n_heads // hps),
        in_specs=[seq_spec(col0), seq_spec(col0 + n_heads), seq_spec(col0 + 2 * n_heads),
                  seq_spec(col0 + 3 * n_heads),
                  pl.BlockSpec((1, t, LANES), lambda bi, h: (bi, 0, 0)),
                  cw_spec(0), cw_spec(n_heads), cw_spec(2 * n_heads), small],
        out_specs=[pl.BlockSpec((1, t, wide), lambda bi, h: (bi, 0, h)),
                   pl.BlockSpec((1, hps, dh, dh), lambda bi, h: (bi, h, 0, 0))],
        out_shape=[jax.ShapeDtypeStruct((b, t, n_heads * dh), BF16),
                   jax.ShapeDtypeStruct((b, n_heads, dh, dh), F32)],
        scratch_shapes=[chunk_f32, chunk_f32, chunk_f32, chunk_f32, chunk_f32,
                        pltpu.VMEM((hps, nc, dh + c, 2 * dh), BF16),
                        pltpu.VMEM((hps, nc, dh, dh), F32),
                        pltpu.VMEM((hps, nc, c, dh), F32),
                        pltpu.VMEM((hps, nc, 1, dh), F32),
                        pltpu.VMEM((hps, nc, c, dh), F32)],
        compiler_params=pltpu.CompilerParams(
            dimension_semantics=("parallel", "parallel"), vmem_limit_bytes=VMEM_LIMIT),
        name="gdn_prompt",
    )(y3, y3, y3, y3, gb3, conv_w, conv_w, conv_w, gdn_norm)


def _gdn_sample_kernel(y_ref, gb_ref, cs_ref, st_ref, cw_ref, gn_ref, yo_ref, cn_ref, so_ref, *, n_heads, col0):
    dh = GDN_HEAD_DIM
    kw = n_heads * dh
    row = y_ref[0]
    xg = row[:, col0:col0 + 3 * kw]
    zg = row[:, col0 + 3 * kw:col0 + 4 * kw]
    buf = cs_ref[0]
    cw = cw_ref[...]
    yc = buf[0:1] * cw[0:1]
    for tap in range(1, CONV_W - 1):
        yc = yc + buf[tap:tap + 1] * cw[tap:tap + 1]
    yc = yc + xg * cw[CONV_W - 1:CONV_W]
    act = _silu(yc)
    for tap in range(CONV_W - 2):
        cn_ref[0, tap:tap + 1, :] = buf[tap + 1:tap + 2]
    cn_ref[0, CONV_W - 2:CONV_W - 1, :] = xg

    g_all = gb_ref[0]
    beta_all = g_all
    eye =lax.broadcasted_iota(jnp.int32, (dh, dh), 0) == lax.broadcasted_iota(jnp.int32, (dh, dh), 1)

    def to_col(r):
        return jnp.sum(jnp.where(eye, jnp.broadcast_to(r, (dh, dh)), 0.0), axis=1, keepdims=True)

    def l2n(x):
        return x * lax.rsqrt(jnp.sum(x * x, axis=-1, keepdims=True) + NORM_EPS)

    gn = gn_ref[...]
    for h in range(n_heads):
        qh = l2n(act[:, h * dh:(h + 1) * dh]) * (dh ** -0.5)
        kh = l2n(act[:, kw + h * dh:kw + (h + 1) * dh])
        vh = act[:, 2 * kw + h * dh:2 * kw + (h + 1) * dh]
        g_h = g_all[:, h:h + 1]
        beta_h = beta_all[:, n_heads + h:n_heads + h + 1]
        s = st_ref[0, h] * jnp.exp(g_h)
        kcol = to_col(kh)
        ks = jnp.sum(kcol * s, axis=0, keepdims=True)
        delta = beta_h * (vh - ks)
        s = s + kcol * delta
        so_ref[0, h] = s
        o = jnp.sum(to_col(qh) * s, axis=0, keepdims=True)
        yo_ref[0, :, h * dh:(h + 1) * dh] = (
            _rms(o, gn) * _silu(zg[:, h * dh:(h + 1) * dh])).astype(yo_ref.dtype)


def _gdn_sample(ys3, gb3, conv_state, gdn_state, conv_w, gdn_norm, *, n_heads, col0):
    db, _, n = ys3.shape
    dh = GDN_HEAD_DIM
    kw = n_heads * dh
    kern = functools.partial(_gdn_sample_kernel, n_heads=n_heads, col0=col0)
    small = pl.BlockSpec((1, LANES), lambda b: (0, 0))
    return pl.pallas_call(
        kern,
        grid=(db,),
        in_specs=[pl.BlockSpec((1, 1, n), lambda b: (b, 0, 0)),
                  pl.BlockSpec((1, 1, LANES), lambda b: (b, 0, 0)),
                  pl.BlockSpec((1, CONV_W - 1, 3 * kw), lambda b: (b, 0, 0)),
                  pl.BlockSpec((1, n_heads, dh, dh), lambda b: (b, 0, 0, 0)),
                  pl.BlockSpec((CONV_W, 3 * kw), lambda b: (0, 0)),
                  small],
        out_specs=[pl.BlockSpec((1, 1, kw), lambda b: (b, 0, 0)),
                   pl.BlockSpec((1, CONV_W - 1, 3 * kw), lambda b: (b, 0, 0)),
                   pl.BlockSpec((1, n_heads, dh, dh), lambda b: (b, 0, 0, 0))],
        out_shape=[jax.ShapeDtypeStruct((db, 1, kw), BF16),
                   jax.ShapeDtypeStruct((db, CONV_W - 1, 3 * kw), F32),
                   jax.ShapeDtypeStruct((db, n_heads, dh, dh), F32)],
        compiler_params=pltpu.CompilerParams(
            dimension_semantics=("parallel",), vmem_limit_bytes=VMEM_LIMIT),
        name="gdn_sample",
    )(ys3, gb3, conv_state, gdn_state, conv_w, gdn_norm)


def _merge_kernel(ys_ref, yg_ref, ms_ref, mg_ref, x_ref, wus_ref, wug_ref, wo_ref, h_ref):
    merged = (_sigmoid(ms_ref[...]) * _dot(ys_ref[...], wus_ref[...])
              + _sigmoid(mg_ref[...]) * _dot(yg_ref[...], wug_ref[...]))
    h_ref[...] = x_ref[...] + _dot(merged.astype(BF16), wo_ref[...])


def _merge(ys, yg, y, x, w_up_sba, w_up_gdn, w_o, *, tm, gate_col):
    m, d = x.shape
    kw = ys.shape[1]

    def resident(shape):
        return pl.BlockSpec(shape, lambda i: (0, 0), pipeline_mode=pl.Buffered(1))

    return pl.pallas_call(
        _merge_kernel,
        grid=(m // tm,),
        in_specs=[pl.BlockSpec((tm, kw), lambda i: (i, 0)),
                  pl.BlockSpec((tm, kw), lambda i: (i, 0)),
                  pl.BlockSpec((tm, d), lambda i: (i, gate_col)),
                  pl.BlockSpec((tm, d), lambda i: (i, gate_col + 1)),
                  pl.BlockSpec((tm, d), lambda i: (i, 0)),
                  resident(w_up_sba.shape), resident(w_up_gdn.shape), resident(w_o.shape)],
        out_specs=pl.BlockSpec((tm, d), lambda i: (i, 0)),
        out_shape=jax.ShapeDtypeStruct((m, d), F32),
        compiler_params=pltpu.CompilerParams(
            dimension_semantics=("parallel",), vmem_limit_bytes=VMEM_LIMIT),
        name="merge_out",
    )(ys, yg, y, y, x, w_up_sba, w_up_gdn, w_o)


def _ple_kernel(h_ref, p_ref, np_ref, nf_ref, wg_ref, wp_ref, o_ref, *, final_norm):
    h = h_ref[...]
    gate = _sigmoid(_dot(_rms(h, np_ref[...]).astype(BF16), wg_ref[...]))
    h = h + gate * _dot(p_ref[...].astype(BF16), wp_ref[...])
    o_ref[...] = _rms(h, nf_ref[...]) if final_norm else h


def _ple(h, p, norm_ple, norm_final, w_gate, w_proj, *, tm, final_norm):
    m, d = h.shape
    pd = p.shape[1]

    def resident(shape):
        return pl.BlockSpec(shape, lambda i: (0, 0), pipeline_mode=pl.Buffered(1))

    return pl.pallas_call(
        functools.partial(_ple_kernel, final_norm=final_norm),
        grid=(m // tm,),
        in_specs=[pl.BlockSpec((tm, d), lambda i: (i, 0)),
                  pl.BlockSpec((tm, pd), lambda i: (i, 0)),
                  resident((1, d)), resident((1, d)), resident(w_gate.shape), resident(w_proj.shape)],
        out_specs=pl.BlockSpec((tm, d), lambda i: (i, 0)),
        out_shape=jax.ShapeDtypeStruct((m, d), F32),
        compiler_params=pltpu.CompilerParams(
            dimension_semantics=("parallel",), vmem_limit_bytes=VMEM_LIMIT),
        name="ple_norm",
    )(h, p, norm_ple, norm_final, w_gate, w_proj)


def _pad_row(v):
    return jnp.zeros((1, LANES), F32).at[0, :v.shape[0]].set(v.astype(F32))


def kernel(x_prompt, x_sample, cache_k, cache_v, state_conv, state_gdn, page_table, p_prompt, p_sample, norm_mix, w_in, sba_bias, conv_w, gdn_a_log, gdn_dt_bias, gdn_norm, w_up_sba, w_up_gdn, w_o, norm_ple, w_ple_gate, w_ple_proj, norm_final):
    depth = w_in.shape[0]
    b, t, d = x_prompt.shape
    db, dt, _ = x_sample.shape
    assert dt == 1
    n_sba = sba_bias.shape[1]
    n_gdn = gdn_a_log.shape[1]
    sba_w = n_sba * SBA_HEAD_DIM
    gdn_w = n_gdn * GDN_HEAD_DIM
    conv_ch = 3 * gdn_w
    ab0 = 4 * sba_w + conv_ch
    n_main = w_in.shape[2] - 2 * n_gdn
    gdn_col0 = 4 * sba_w // LANES
    gate_col = (4 * sba_w + conv_ch + gdn_w) // d
    page = cache_k.shape[2]
    ck = jnp.transpose(cache_k, (0, 1, 3, 4, 2)).reshape(cache_k.shape[0], cache_k.shape[1], sba_w, page)
    cv = jnp.transpose(cache_v, (0, 1, 3, 4, 2)).reshape(cache_v.shape[0], cache_v.shape[1], sba_w, page)

    hp = x_prompt.reshape(b * t, d)
    hs = x_sample.reshape(db, d)
    outs = [[] for _ in range(8)]
    for layer in range(depth):
        w_l = w_in[layer]
        w_main = jnp.concatenate([w_l[:, :ab0], w_l[:, ab0 + 2 * n_gdn:]], axis=1).astype(BF16)
        w_ab = jnp.zeros((d, LANES), BF16).at[:, :2 * n_gdn].set(w_l[:, ab0:ab0 + 2 * n_gdn].astype(BF16))
        gain = norm_mix[layer].reshape(1, d)
        wus = w_up_sba[layer].astype(BF16)
        wug = w_up_gdn[layer].astype(BF16)
        wo = w_o[layer].astype(BF16)
        wpg = w_ple_gate[layer].astype(BF16)
        wpp = w_ple_proj[layer].astype(BF16)
        alog_pad = _pad_row(gdn_a_log[layer])
        dtb_pad = _pad_row(gdn_dt_bias[layer])
        gnorm = gdn_norm[layer].reshape(1, GDN_HEAD_DIM)
        nple = norm_ple[layer].reshape(1, d)
        nfin = norm_final.reshape(1, d)
        last = layer == depth - 1

        y_p, gb_p = _norm_matmul(hp, gain, w_main, w_ab, alog_pad, dtb_pad, n_gdn=n_gdn,
                                 tm=_tile(b * t, 1024), tn=_tile(n_main, 512))
        y3 = y_p.reshape(b, t, n_main)
        ys_p, k_p, v_p = _sba_prompt(y3, sba_bias[layer].astype(F32), n_heads=n_sba)
        yg_p, s_p = _gdn_prompt(y3, gb_p.reshape(b, t, LANES), conv_w[layer], gnorm,
                                n_heads=n_gdn, col0=gdn_col0)
        h_p = _merge(ys_p.reshape(b * t, sba_w), yg_p.reshape(b * t, gdn_w), y_p, hp, wus, wug, wo,
                     tm=_tile(b * t, 256), gate_col=gate_col)
        hp = _ple(h_p, p_prompt[layer].reshape(b * t, -1), nple, nfin, wpg, wpp, tm=_tile(b * t, 512),
                  final_norm=last)
        outs[0].append(k_p.reshape(b, t, n_sba, SBA_HEAD_DIM))
        outs[1].append(v_p.reshape(b, t, n_sba, SBA_HEAD_DIM))
        outs[2].append(y3[:, t - (CONV_W - 1):, 4 * sba_w:4 * sba_w + conv_ch])
        outs[3].append(s_p)

        y_s, gb_s = _norm_matmul(hs, gain, w_main, w_ab, alog_pad, dtb_pad, n_gdn=n_gdn,
                                 tm=db, tn=_tile(n_main, 512))
        ys3 = y_s.reshape(db, 1, n_main)
        ys_s = _sba_sample(ys3, sba_bias[layer].astype(F32).reshape(n_sba, 1), ck, cv, page_table, layer,
                           n_heads=n_sba, pages_per_step=_tile(page_table.shape[1], 16))
        yg_s, cn_s, s_s = _gdn_sample(ys3, gb_s.reshape(db, 1, LANES), state_conv[layer], state_gdn[layer],
                                      conv_w[layer], gnorm, n_heads=n_gdn, col0=4 * sba_w)
        h_s = _merge(ys_s.reshape(db, sba_w), yg_s.reshape(db, gdn_w), y_s, hs, wus, wug, wo,
                     tm=db, gate_col=gate_col)
        hs = _ple(h_s, p_sample[layer].reshape(db, -1), nple, nfin, wpg, wpp, tm=db, final_norm=last)
        outs[4].append(ys3[:, :, sba_w:2 * sba_w].reshape(db, 1, n_sba, SBA_HEAD_DIM))
        outs[5].append(ys3[:, :, 2 * sba_w:3 * sba_w].reshape(db, 1, n_sba, SBA_HEAD_DIM))
        outs[6].append(cn_s)
        outs[7].append(s_s)

    return (hp.reshape(b, t, d), hs.reshape(db, 1, d)) + tuple(jnp.stack(o) for o in outs)
```

```python
import functools

import jax
import jax.numpy as jnp
from jax import lax
from jax.experimental import pallas as pl
from jax.experimental.pallas import tpu as pltpu

F32 = jnp.float32
BF16 = jnp.bfloat16
NORM_EPS = 1e-6
LOG2E = 1.4426950408889634
LANES = 128
SBA_HEAD_DIM = 64
GDN_HEAD_DIM = 128
GDN_CHUNK = 64
GDN_PREP_GROUP = 16
CONV_W = 4
VMEM_LIMIT = 56 * 1024 * 1024


def _tile(n, preferred):
    t = min(preferred, n)
    while n % t:
        t //= 2
    return t


def _softplus(x):
    return jnp.maximum(x, 0.0) + jnp.log1p(jnp.exp(-jnp.abs(x)))


def _sigmoid(x):
    return 0.5 + 0.5 * jnp.tanh(0.5 * x)


def _silu(x):
    h = 0.5 * x
    return h + h * jnp.tanh(h)


def _rms(x, gain):
    ms = jnp.mean(x * x, axis=-1, keepdims=True)
    return x * lax.rsqrt(ms + NORM_EPS) * gain


def _dot(a, b):
    return jnp.dot(a, b, preferred_element_type=F32)


def _dot_nt(a, b):
    return lax.dot_general(a, b, (((1,), (1,)), ((), ())), preferred_element_type=F32)


def _split_hi_lo(x):
    hi = x.astype(BF16)
    lo = (x - hi.astype(F32)).astype(BF16)
    return hi, lo


def _norm_matmul_kernel(x_ref, g_ref, w_ref, wab_ref, alog_ref, dtb_ref, y_ref, gb_ref, u_ref, *, n_gdn):
    @pl.when(pl.program_id(1) == 0)
    def _():
        u = _rms(x_ref[...], g_ref[...]).astype(BF16)
        u_ref[...] = u
        ab = _dot(u, wab_ref[...])
        lane = lax.broadcasted_iota(jnp.int32, ab.shape, 1)
        g = -jnp.exp(alog_ref[...]) * _softplus(ab + dtb_ref[...])
        gb_ref[...] = jnp.where(lane < n_gdn, g, _sigmoid(ab))

    y_ref[...] = _dot(u_ref[...], w_ref[...])


def _norm_matmul(x, gain, w, w_ab, alog_pad, dtb_pad, *, n_gdn, tm, tn):
    m, d = x.shape
    n = w.shape[1]
    assert m % tm == 0 and n % tn == 0
    return pl.pallas_call(
        functools.partial(_norm_matmul_kernel, n_gdn=n_gdn),
        grid=(m // tm, n // tn),
        in_specs=[
            pl.BlockSpec((tm, d), lambda i, j: (i, 0)),
            pl.BlockSpec((1, d), lambda i, j: (0, 0)),
            pl.BlockSpec((d, tn), lambda i, j: (0, j)),
            pl.BlockSpec((d, LANES), lambda i, j: (0, 0)),
            pl.BlockSpec((1, LANES), lambda i, j: (0, 0)),
            pl.BlockSpec((1, LANES), lambda i, j: (0, 0)),
        ],
        out_specs=[
            pl.BlockSpec((tm, tn), lambda i, j: (i, j)),
            pl.BlockSpec((tm, LANES), lambda i, j: (i, 0)),
        ],
        out_shape=[jax.ShapeDtypeStruct((m, n), F32), jax.ShapeDtypeStruct((m, LANES), F32)],
        scratch_shapes=[pltpu.VMEM((tm, d), BF16)],
        compiler_params=pltpu.CompilerParams(
            dimension_semantics=("parallel", "arbitrary"), vmem_limit_bytes=VMEM_LIMIT),
        name="norm_inproj",
    )(x, gain, w, w_ab, alog_pad, dtb_pad)


def _softplus2(y):
    neg_abs = lax.bitcast_convert_type(lax.bitcast_convert_type(y, jnp.uint32) | jnp.uint32(0x80000000), F32)
    return jnp.maximum(y, 0.0) + jnp.log2(1.0 + jnp.exp2(neg_abs))


def _cumsum_total_matrix(n):
    j = lax.broadcasted_iota(jnp.int32, (2 * n, 2 * n), 0)
    j = jnp.where(j >= n, j - n, j)
    s = lax.broadcasted_iota(jnp.int32, (2 * n, 2 * n), 1)
    return ((j >= s) | (s >= n)).astype(BF16)


def _sba_prompt_kernel(bias_ref, q_ref, k_ref, v_ref, gate_ref, o_ref, ko_ref, vo_ref,
                       kb_ref, vb_ref, y_s, w_s, carry_s, acc_s, *, tq, tk, scale):
    hd = SBA_HEAD_DIM
    nh = q_ref.shape[-1] // hd
    r = tq // tk
    g = pl.program_id(1)
    i = pl.program_id(2)

    @pl.when(i == 0)
    def _():
        ko_ref[...] = k_ref[...]
        vo_ref[...] = v_ref[...]
        for h in range(nh):
            kb_ref[h] = k_ref[0, :, h * hd:(h + 1) * hd].astype(BF16)
            vb_ref[h] = v_ref[0, :, h * hd:(h + 1) * hd].astype(BF16)

    q = (q_ref[0] * (scale * LOG2E)).astype(BF16)
    qs = [q[:, h * hd:(h + 1) * hd] for h in range(nh)]
    bias = [bias_ref[g * nh + h] * LOG2E for h in range(nh)]
    cmat = _cumsum_total_matrix(tk)
    row = lax.broadcasted_iota(jnp.int32, (tq, tk), 0)
    col = lax.broadcasted_iota(jnp.int32, (tq, tk), 1)
    q0 = i * tq
    heads = range(nh)
    tiles = [(n, h) for n in range(r) for h in heads]
    tid = {(n, h): n * nh + h for n, h in tiles}

    def starts(grp):
        return [pl.multiple_of(((grp + 1) * r - 1 - n) * tk, tk) for n in range(r)]

    def qk_tile(st, slot, t_):
        n, h = t_
        y_s[slot, tid[t_]] = _dot_nt(qs[h], kb_ref[h, pl.ds(st[n], tk), :]) + bias[h]

    def pv_tile(st, t_):
        n, h = t_
        return _dot(w_s[tid[t_]], vb_ref[h, pl.ds(st[n], tk), :])

    def cumsum_tile(y, keep):
        sp = _softplus2(y)
        if keep is not None:
            sp = jnp.where(keep, sp, 0.0)
        return _dot(jnp.concatenate(_split_hi_lo(sp), axis=1), cmat)

    def accumulate(pv):
        for h in heads:
            total = pv[0, h]
            for n in range(1, r):
                total = total + pv[n, h]
            acc_s[h] = acc_s[h] + total

    def weights(y, ct, keep):
        for n in range(r):
            for h in heads:
                x = jnp.exp2(y[n, h] - ct[n, h][:, :tk] - carry_s[h])
                w_s[tid[n, h]] = (x if keep is None else jnp.where(keep[n], x, 0.0)).astype(BF16)
            for h in heads:
                carry_s[h] = carry_s[h] + ct[n, h][:, tk:]

    def group(grp, slot, prev, masked):
        st = starts(grp)
        st_next = starts(jnp.maximum(grp - 1, 0))
        st_prev = None if prev is None else starts(prev)
        keep = [(st[n] + col) < (q0 + row) for n in range(r)] if masked else None
        y, ct, pv = {}, {}, {}
        for t_ in tiles:
            y[t_] = y_s[slot, tid[t_]]
            ct[t_] = cumsum_tile(y[t_], None if keep is None else keep[t_[0]])
            qk_tile(st_next, 1 - slot, t_)
            if prev is not None:
                pv[t_] = pv_tile(st_prev, t_)
        weights(y, ct, keep)
        if prev is not None:
            accumulate(pv)

    carry_s[...] = jnp.zeros_like(carry_s)
    acc_s[...] = jnp.zeros_like(acc_s)
    st_diag = starts(i)
    for t_ in tiles:
        qk_tile(st_diag, 0, t_)
    group(i, 0, None, True)

    def body(t, _):
        group(i - 1 - t, (t + 1) % 2, i - t, False)
        return 0

    lax.fori_loop(0, i, body, 0)
    st_last = starts(0)
    accumulate({t_: pv_tile(st_last, t_) for t_ in tiles})
    o = jnp.concatenate([acc_s[h] for h in heads], axis=1)
    o_ref[0] = (o * _silu(gate_ref[0])).astype(o_ref.dtype)


def _sba_prompt(y3, bias, *, n_heads, tq=256, tk=128, width=256):
    b, t, _ = y3.shape
    width = _tile(n_heads * SBA_HEAD_DIM, width)
    ng = n_heads * SBA_HEAD_DIM // width
    nh = width // SBA_HEAD_DIM
    tq = _tile(t, tq)
    n_tiles = (tq // tk) * nh
    kern = functools.partial(_sba_prompt_kernel, tq=tq, tk=tk, scale=SBA_HEAD_DIM ** -0.5)
    return pl.pallas_call(
        kern,
        grid=(b, ng, t // tq),
        in_specs=[
            pl.BlockSpec(memory_space=pltpu.SMEM),
            pl.BlockSpec((1, tq, width), lambda bi, h, i: (bi, i, h)),
            pl.BlockSpec((1, t, width), lambda bi, h, i: (bi, 0, ng + h)),
            pl.BlockSpec((1, t, width), lambda bi, h, i: (bi, 0, 2 * ng + h)),
            pl.BlockSpec((1, tq, width), lambda bi, h, i: (bi, i, 3 * ng + h)),
        ],
        out_specs=[pl.BlockSpec((1, tq, width), lambda bi, h, i: (bi, i, h)),
                   pl.BlockSpec((1, t, width), lambda bi, h, i: (bi, 0, h)),
                   pl.BlockSpec((1, t, width), lambda bi, h, i: (bi, 0, h))],
        out_shape=[jax.ShapeDtypeStruct((b, t, n_heads * SBA_HEAD_DIM), BF16),
                   jax.ShapeDtypeStruct((b, t, n_heads * SBA_HEAD_DIM), F32),
                   jax.ShapeDtypeStruct((b, t, n_heads * SBA_HEAD_DIM), F32)],
        scratch_shapes=[pltpu.VMEM((nh, t, SBA_HEAD_DIM), BF16), pltpu.VMEM((nh, t, SBA_HEAD_DIM), BF16),
                        pltpu.VMEM((2, n_tiles, tq, tk), F32), pltpu.VMEM((n_tiles, tq, tk), BF16),
                        pltpu.VMEM((nh, tq, tk), F32), pltpu.VMEM((nh, tq, SBA_HEAD_DIM), F32)],
        compiler_params=pltpu.CompilerParams(
            dimension_semantics=("parallel", "parallel", "arbitrary"), vmem_limit_bytes=VMEM_LIMIT),
        name="sba_prompt",
    )(bias, y3, y3, y3, y3)


def _sba_sample_kernel(pt_ref, q_ref, kn_ref, vn_ref, gate_ref, bias_ref, *rest, pages_per_step, past, scale):
    del pt_ref
    pp = pages_per_step
    k_refs = rest[:pp]
    v_refs = rest[pp:2 * pp]
    o_ref, oacc_ref, carry_ref, qbd_ref = rest[2 * pp:]
    hd = SBA_HEAD_DIM
    nh, width = qbd_ref.shape
    page = k_refs[0].shape[1]
    j = pl.program_id(1)
    head_of_lane = lax.broadcasted_iota(jnp.int32, (nh, width), 1) // hd
    own = head_of_lane == lax.broadcasted_iota(jnp.int32, (nh, width), 0)

    @pl.when(j == 0)
    def _():
        qrow = q_ref[0] * scale
        qbd_ref[...] = jnp.where(own, jnp.broadcast_to(qrow, (nh, width)), 0.0).astype(BF16)
        oacc_ref[...] = jnp.zeros_like(oacc_ref)
        carry_ref[...] = jnp.zeros_like(carry_ref)

    cmat = _cumsum_total_matrix(page)
    qbd = qbd_ref[...]
    bias = bias_ref[...]
    carry = carry_ref[...]
    slots = range(pp)
    z = [_dot(qbd, k_refs[s][...].astype(BF16)) + bias for s in slots]
    sp = [_softplus(z[s]) for s in slots]
    ct = [_dot(jnp.concatenate(_split_hi_lo(sp[s]), axis=1), cmat) for s in slots]
    w = []
    for s in slots:
        w.append(jnp.exp(z[s] - ct[s][:, :page] - carry).astype(BF16))
        carry = carry + ct[s][:, page:]
    pv = [_dot_nt(w[s], v_refs[s][...].astype(BF16)) for s in slots]
    oacc = oacc_ref[...]
    for s in slots:
        oacc = oacc + pv[s]
    oacc_ref[...] = oacc
    carry_ref[...] = carry

    @pl.when(j == pl.num_programs(1) - 1)
    def _():
        o = jnp.sum(jnp.where(own, oacc, 0.0), axis=0, keepdims=True)
        kn = kn_ref[0].astype(BF16).astype(F32)
        z_new = jnp.sum(qbd.astype(F32) * kn, axis=-1, keepdims=True) + bias
        q_pos = jnp.full(z_new.shape, past, jnp.int32)
        k_pos = jnp.full(z_new.shape, past, jnp.int32)
        w_new = jnp.where(k_pos < q_pos, jnp.exp(z_new - _softplus(z_new)), 0.0)
        w_new = w_new.astype(BF16).astype(F32)
        w_lane = jnp.sum(jnp.where(own, jnp.broadcast_to(w_new, (nh, width)), 0.0), axis=0, keepdims=True)
        o = o + w_lane * vn_ref[0].astype(BF16).astype(F32)
        o_ref[0] = (o * _silu(gate_ref[0])).astype(o_ref.dtype)


def _sba_sample(ys3, bias_col, cache_k, cache_v, page_table, layer, *, n_heads, pages_per_step=8):
    db = ys3.shape[0]
    width = n_heads * SBA_HEAD_DIM
    n_pages = page_table.shape[1]
    page = cache_k.shape[3]
    pp = pages_per_step
    assert n_pages % pp == 0
    kern = functools.partial(_sba_sample_kernel, pages_per_step=pp, past=n_pages * page,
                             scale=SBA_HEAD_DIM ** -0.5)

    def row_spec(col):
        return pl.BlockSpec((1, 1, width), lambda b, j, pt: (b, 0, col))

    def page_spec(s):
        return pl.BlockSpec((None, None, width, page),
                            lambda b, j, pt: (layer, pt[b, n_pages - 1 - (j * pp + s)], 0, 0))

    grid_spec = pltpu.PrefetchScalarGridSpec(
        num_scalar_prefetch=1,
        grid=(db, n_pages // pp),
        in_specs=[row_spec(0), row_spec(1), row_spec(2), row_spec(3),
                  pl.BlockSpec((n_heads, 1), lambda b, j, pt: (0, 0))]
        + [page_spec(s) for s in range(pp)] + [page_spec(s) for s in range(pp)],
        out_specs=pl.BlockSpec((1, 1, width), lambda b, j, pt: (b, 0, 0)),
        scratch_shapes=[pltpu.VMEM((n_heads, width), F32), pltpu.VMEM((n_heads, page), F32),
                        pltpu.VMEM((n_heads, width), BF16)],
    )
    return pl.pallas_call(
        kern,
        grid_spec=grid_spec,
        out_shape=jax.ShapeDtypeStruct((db, 1, width), BF16),
        compiler_params=pltpu.CompilerParams(
            dimension_semantics=("parallel", "arbitrary"), vmem_limit_bytes=VMEM_LIMIT),
        name="sba_sample",
    )(page_table, ys3, ys3, ys3, ys3, bias_col, *([cache_k] * pp), *([cache_v] * pp))


def _bdot_nt(a, b):
    return jnp.einsum('gmk,gnk->gmn', a, b, preferred_element_type=F32)


def _bdot_split(a, b):
    a_hi, a_lo = _split_hi_lo(a)
    b_hi, b_lo = _split_hi_lo(b)
    lhs = jnp.concatenate([a_hi, a_lo, a_hi], axis=2)
    rhs = jnp.concatenate([b_hi, b_hi, b_lo], axis=1)
    return jnp.einsum('gmk,gkn->gmn', lhs, rhs, preferred_element_type=F32)


def _unit_lower_inverse(a):
    n = a.shape[-1]
    row = lax.broadcasted_iota(jnp.int32, (n, 2 * n), 0)
    col = lax.broadcasted_iota(jnp.int32, (n, 2 * n), 1)
    right = col >= n
    p = jnp.concatenate([-a, jnp.zeros_like(a)], axis=2) + (col - n == row).astype(F32)
    span = 1
    while span < n:
        p = _bdot_split(p[:, :, :n], p) + jnp.where(right, p, 0.0)
        span *= 2
    return p[:, :, n:]


def _gdn_prompt_kernel(xq_ref, xk_ref, xv_ref, zg_ref, gb_ref, cwq_ref, cwk_ref, cwv_ref, gn_ref, y_ref, s_ref,
                       k_s, q_s, kb_s, vb_s, g_s, gq_s, n_s, op_s, eg_s, o_s, *, n_heads):
    c = GDN_CHUNK
    dh = GDN_HEAD_DIM
    hps = xq_ref.shape[2] // dh
    h0 = pl.program_id(1) * hps
    t = xq_ref.shape[1]
    nc = t // c
    group = _tile(nc, GDN_PREP_GROUP)
    rows = lax.broadcasted_iota(jnp.int32, (t, dh), 0)
    lane = lax.broadcasted_iota(jnp.int32, (t, dh), 1)
    ri = lax.broadcasted_iota(jnp.int32, (c, c), 0)
    ci = lax.broadcasted_iota(jnp.int32, (c, c), 1)
    causal = ri >= ci
    strict = ri > ci
    diag = ri == ci
    tril3 = jnp.broadcast_to(jnp.concatenate([causal.astype(BF16)] * 3, axis=1), (group, c, 3 * c))

    def l2n(x):
        return x * lax.rsqrt(jnp.sum(x * x, axis=-1, keepdims=True) + NORM_EPS)

    for hh in range(hps):
        lanes = slice(hh * dh, (hh + 1) * dh)

        def conv_silu(x_ref, cw_ref):
            x = x_ref[0, :, lanes]
            cw = cw_ref[:, lanes]
            y = None
            for tap in range(CONV_W):
                shift = CONV_W - 1 - tap
                xs = x if shift == 0 else jnp.where(rows >= shift, pltpu.roll(x, shift, axis=0), 0.0)
                term = xs * cw[tap:tap + 1, :]
                y = term if y is None else y + term
            return _silu(y)

        q = l2n(conv_silu(xq_ref, cwq_ref)) * (dh ** -0.5)
        k = l2n(conv_silu(xk_ref, cwk_ref))
        v = conv_silu(xv_ref, cwv_ref)

        gb = gb_ref[0]
        g = jnp.sum(jnp.where(lane == h0 + hh, gb, 0.0), axis=-1, keepdims=True)
        beta = jnp.sum(jnp.where(lane == n_heads + h0 + hh, gb, 0.0), axis=-1, keepdims=True)

        k_s[...] = k.reshape(nc, c, dh)
        q_s[...] = q.reshape(nc, c, dh)
        kb_s[...] = (k * beta).reshape(nc, c, dh)
        vb_s[...] = (v * beta).reshape(nc, c, dh)
        g_s[...] = jnp.broadcast_to(g, (t, dh)).reshape(nc, c, dh)

        def prep(n, _):
            sl = pl.ds(pl.multiple_of(n * group, group), group)
            k_c = k_s[sl]
            q_c = q_s[sl]
            kb_c = kb_s[sl]
            g_c = g_s[sl]
            g_hi = g_c.astype(BF16)
            g_rest = g_c - g_hi.astype(F32)
            g_mid, g_lo = _split_hi_lo(g_rest)
            cg_c = jnp.einsum('gmk,gkn->gmn', tril3, jnp.concatenate([g_hi, g_mid, g_lo], axis=1),
                              preferred_element_type=F32)
            cg_col = cg_c[:, :, :c]
            cg_row = jnp.sum(jnp.where(diag, cg_col, 0.0), axis=1, keepdims=True)
            decay = jnp.where(causal, jnp.exp(cg_col - cg_row), 0.0)
            kbq = _bdot_nt(jnp.concatenate([kb_c, q_c], axis=1).astype(BF16), k_c.astype(BF16))
            a_mat = jnp.where(strict, kbq[:, :c] * decay, 0.0)
            attn = jnp.where(causal, kbq[:, c:] * decay, 0.0)
            tinv = _unit_lower_inverse(a_mat)
            ecg = jnp.exp(cg_c)
            g_last = cg_c[:, c - 1:c, :]
            uw = _bdot_split(tinv, jnp.concatenate([vb_s[sl], kb_c * ecg], axis=2))
            u_hi, u_lo = _split_hi_lo(uw[:, :, :dh])
            rhs = jnp.concatenate([uw[:, :, dh:].astype(BF16), u_hi, u_lo], axis=2)
            kd_t = jnp.swapaxes(k_c * jnp.exp(g_last - cg_c), 1, 2).astype(BF16)
            kar = jnp.einsum('gmc,gcn->gmn', jnp.concatenate([kd_t, attn.astype(BF16)], axis=1), rhs,
                             preferred_element_type=F32)
            kr = kar[:, :dh]
            ar = kar[:, dh:]
            gq = jnp.concatenate([kr[:, :, :dh], q_c * ecg - ar[:, :, :dh]], axis=1)
            gq_s[hh, sl] = jnp.concatenate(_split_hi_lo(gq), axis=2)
            n_s[hh, sl] = kr[:, :, dh:2 * dh] + kr[:, :, 2 * dh:]
            op_s[hh, sl] = ar[:, :, dh:2 * dh] + ar[:, :, 2 * dh:]
            eg_s[hh, sl] = jnp.exp(g_last)
            return 0

        lax.fori_loop(0, nc // group, prep, 0)

    def scan(n, states):
        sb = [s.astype(BF16) for s in states]
        gs_qs = [_dot(gq_s[hh, n], jnp.concatenate([sb[hh], sb[hh]], axis=0)) for hh in range(hps)]
        for hh in range(hps):
            o_s[hh, n] = gs_qs[hh][dh:] + op_s[hh, n]
        return tuple(states[hh] * eg_s[hh, n] + n_s[hh, n] - gs_qs[hh][:dh] for hh in range(hps))

    s_fin = lax.fori_loop(0, nc, scan, tuple(jnp.zeros((dh, dh), F32) for _ in range(hps)))
    for hh in range(hps):
        lanes = slice(hh * dh, (hh + 1) * dh)
        s_ref[0, hh] = s_fin[hh]
        o = o_s[hh].reshape(t, dh)
        y_ref[0, :, lanes] = (_rms(o, gn_ref[...]) * _silu(zg_ref[0, :, lanes])).astype(y_ref.dtype)


def _gdn_prompt(y3, gb3, conv_w, gdn_norm, *, n_heads, col0, heads_per_step=2):
    b, t, _ = y3.shape
    dh = GDN_HEAD_DIM
    c = GDN_CHUNK
    nc = t // c
    hps = heads_per_step
    assert n_heads % hps == 0 and col0 % hps == 0
    wide = hps * dh
    kern = functools.partial(_gdn_prompt_kernel, n_heads=n_heads)

    def seq_spec(off):
        return pl.BlockSpec((1, t, wide), lambda bi, h: (bi, 0, off // hps + h))

    def cw_spec(off):
        return pl.BlockSpec((CONV_W, wide), lambda bi, h: (0, off // hps + h))

    small = pl.BlockSpec((1, LANES), lambda bi, h: (0, 0))
    chunk_f32 = pltpu.VMEM((nc, c, dh), F32)
    return pl.pallas_call(
        kern,
        grid=(b, n_heads // hps),
        in_specs=[seq_spec(col0), seq_spec(col0 + n_heads), seq_spec(col0 + 2 * n_heads),
                  seq_spec(col0 + 3 * n_heads),
                  pl.BlockSpec((1, t, LANES), lambda bi, h: (bi, 0, 0)),
                  cw_spec(0), cw_spec(n_heads), cw_spec(2 * n_heads), small],
        out_specs=[pl.BlockSpec((1, t, wide), lambda bi, h: (bi, 0, h)),
                   pl.BlockSpec((1, hps, dh, dh), lambda bi, h: (bi, h, 0, 0))],
        out_shape=[jax.ShapeDtypeStruct((b, t, n_heads * dh), BF16),
                   jax.ShapeDtypeStruct((b, n_heads, dh, dh), F32)],
        scratch_shapes=[chunk_f32, chunk_f32, chunk_f32, chunk_f32, chunk_f32,
                        pltpu.VMEM((hps, nc, dh + c, 2 * dh), BF16),
                        pltpu.VMEM((hps, nc, dh, dh), F32),
                        pltpu.VMEM((hps, nc, c, dh), F32),
                        pltpu.VMEM((hps, nc, 1, dh), F32),
                        pltpu.VMEM((hps, nc, c, dh), F32)],
        compiler_params=pltpu.CompilerParams(
            dimension_semantics=("parallel", "parallel"), vmem_limit_bytes=VMEM_LIMIT),
        name="gdn_prompt",
    )(y3, y3, y3, y3, gb3, conv_w, conv_w, conv_w, gdn_norm)


def _gdn_sample_kernel(y_ref, gb_ref, cs_ref, st_ref, cw_ref, gn_ref, yo_ref, cn_ref, so_ref, *, n_heads, col0):
    dh = GDN_HEAD_DIM
    kw = n_heads * dh
    row = y_ref[0]
    xg = row[:, col0:col0 + 3 * kw]
    zg = row[:, col0 + 3 * kw:col0 + 4 * kw]
    buf = cs_ref[0]
    cw = cw_ref[...]
    yc = buf[0:1] * cw[0:1]
    for tap in range(1, CONV_W - 1):
        yc = yc + buf[tap:tap + 1] * cw[tap:tap + 1]
    yc = yc + xg * cw[CONV_W - 1:CONV_W]
    act = _silu(yc)
    for tap in range(CONV_W - 2):
        cn_ref[0, tap:tap + 1, :] = buf[tap + 1:tap + 2]
    cn_ref[0, CONV_W - 2:CONV_W - 1, :] = xg

    g_all = gb_ref[0]
    beta_all = g_all
    eye =lax.broadcasted_iota(jnp.int32, (dh, dh), 0) == lax.broadcasted_iota(jnp.int32, (dh, dh), 1)

    def to_col(r):
        return jnp.sum(jnp.where(eye, jnp.broadcast_to(r, (dh, dh)), 0.0), axis=1, keepdims=True)

    def l2n(x):
        return x * lax.rsqrt(jnp.sum(x * x, axis=-1, keepdims=True) + NORM_EPS)

    gn = gn_ref[...]
    for h in range(n_heads):
        qh = l2n(act[:, h * dh:(h + 1) * dh]) * (dh ** -0.5)
        kh = l2n(act[:, kw + h * dh:kw + (h + 1) * dh])
        vh = act[:, 2 * kw + h * dh:2 * kw + (h + 1) * dh]
        g_h = g_all[:, h:h + 1]
        beta_h = beta_all[:, n_heads + h:n_heads + h + 1]
        s = st_ref[0, h] * jnp.exp(g_h)
        kcol = to_col(kh)
        ks = jnp.sum(kcol * s, axis=0, keepdims=True)
        delta = beta_h * (vh - ks)
        s = s + kcol * delta
        so_ref[0, h] = s
        o = jnp.sum(to_col(qh) * s, axis=0, keepdims=True)
        yo_ref[0, :, h * dh:(h + 1) * dh] = (
            _rms(o, gn) * _silu(zg[:, h * dh:(h + 1) * dh])).astype(yo_ref.dtype)


def _gdn_sample(ys3, gb3, conv_state, gdn_state, conv_w, gdn_norm, *, n_heads, col0):
    db, _, n = ys3.shape
    dh = GDN_HEAD_DIM
    kw = n_heads * dh
    kern = functools.partial(_gdn_sample_kernel, n_heads=n_heads, col0=col0)
    small = pl.BlockSpec((1, LANES), lambda b: (0, 0))
    return pl.pallas_call(
        kern,
        grid=(db,),
        in_specs=[pl.BlockSpec((1, 1, n), lambda b: (b, 0, 0)),
                  pl.BlockSpec((1, 1, LANES), lambda b: (b, 0, 0)),
                  pl.BlockSpec((1, CONV_W - 1, 3 * kw), lambda b: (b, 0, 0)),
                  pl.BlockSpec((1, n_heads, dh, dh), lambda b: (b, 0, 0, 0)),
                  pl.BlockSpec((CONV_W, 3 * kw), lambda b: (0, 0)),
                  small],
        out_specs=[pl.BlockSpec((1, 1, kw), lambda b: (b, 0, 0)),
                   pl.BlockSpec((1, CONV_W - 1, 3 * kw), lambda b: (b, 0, 0)),
                   pl.BlockSpec((1, n_heads, dh, dh), lambda b: (b, 0, 0, 0))],
        out_shape=[jax.ShapeDtypeStruct((db, 1, kw), BF16),
                   jax.ShapeDtypeStruct((db, CONV_W - 1, 3 * kw), F32),
                   jax.ShapeDtypeStruct((db, n_heads, dh, dh), F32)],
        compiler_params=pltpu.CompilerParams(
            dimension_semantics=("parallel",), vmem_limit_bytes=VMEM_LIMIT),
        name="gdn_sample",
    )(ys3, gb3, conv_state, gdn_state, conv_w, gdn_norm)


def _merge_kernel(ys_ref, yg_ref, ms_ref, mg_ref, x_ref, wus_ref, wug_ref, wo_ref, h_ref):
    merged = (_sigmoid(ms_ref[...]) * _dot(ys_ref[...], wus_ref[...])
              + _sigmoid(mg_ref[...]) * _dot(yg_ref[...], wug_ref[...]))
    h_ref[...] = x_ref[...] + _dot(merged.astype(BF16), wo_ref[...])


def _merge(ys, yg, y, x, w_up_sba, w_up_gdn, w_o, *, tm, gate_col):
    m, d = x.shape
    kw = ys.shape[1]

    def resident(shape):
        return pl.BlockSpec(shape, lambda i: (0, 0), pipeline_mode=pl.Buffered(1))

    return pl.pallas_call(
        _merge_kernel,
        grid=(m // tm,),
        in_specs=[pl.BlockSpec((tm, kw), lambda i: (i, 0)),
                  pl.BlockSpec((tm, kw), lambda i: (i, 0)),
                  pl.BlockSpec((tm, d), lambda i: (i, gate_col)),
                  pl.BlockSpec((tm, d), lambda i: (i, gate_col + 1)),
                  pl.BlockSpec((tm, d), lambda i: (i, 0)),
                  resident(w_up_sba.shape), resident(w_up_gdn.shape), resident(w_o.shape)],
        out_specs=pl.BlockSpec((tm, d), lambda i: (i, 0)),
        out_shape=jax.ShapeDtypeStruct((m, d), F32),
        compiler_params=pltpu.CompilerParams(
            dimension_semantics=("parallel",), vmem_limit_bytes=VMEM_LIMIT),
        name="merge_out",
    )(ys, yg, y, y, x, w_up_sba, w_up_gdn, w_o)


def _ple_kernel(h_ref, p_ref, np_ref, nf_ref, wg_ref, wp_ref, o_ref, *, final_norm):
    h = h_ref[...]
    gate = _sigmoid(_dot(_rms(h, np_ref[...]).astype(BF16), wg_ref[...]))
    h = h + gate * _dot(p_ref[...].astype(BF16), wp_ref[...])
    o_ref[...] = _rms(h, nf_ref[...]) if final_norm else h


def _ple(h, p, norm_ple, norm_final, w_gate, w_proj, *, tm, final_norm):
    m, d = h.shape
    pd = p.shape[1]

    def resident(shape):
        return pl.BlockSpec(shape, lambda i: (0, 0), pipeline_mode=pl.Buffered(1))

    return pl.pallas_call(
        functools.partial(_ple_kernel, final_norm=final_norm),
        grid=(m // tm,),
        in_specs=[pl.BlockSpec((tm, d), lambda i: (i, 0)),
                  pl.BlockSpec((tm, pd), lambda i: (i, 0)),
                  resident((1, d)), resident((1, d)), resident(w_gate.shape), resident(w_proj.shape)],
        out_specs=pl.BlockSpec((tm, d), lambda i: (i, 0)),
        out_shape=jax.ShapeDtypeStruct((m, d), F32),
        compiler_params=pltpu.CompilerParams(
            dimension_semantics=("parallel",), vmem_limit_bytes=VMEM_LIMIT),
        name="ple_norm",
    )(h, p, norm_ple, norm_final, w_gate, w_proj)


def _pad_row(v):
    return jnp.zeros((1, LANES), F32).at[0, :v.shape[0]].set(v.astype(F32))


def kernel(x_prompt, x_sample, cache_k, cache_v, state_conv, state_gdn, page_table, p_prompt, p_sample, norm_mix, w_in, sba_bias, conv_w, gdn_a_log, gdn_dt_bias, gdn_norm, w_up_sba, w_up_gdn, w_o, norm_ple, w_ple_gate, w_ple_proj, norm_final):
    depth = w_in.shape[0]
    b, t, d = x_prompt.shape
    db, dt, _ = x_sample.shape
    assert dt == 1
    n_sba = sba_bias.shape[1]
    n_gdn = gdn_a_log.shape[1]
    sba_w = n_sba * SBA_HEAD_DIM
    gdn_w = n_gdn * GDN_HEAD_DIM
    conv_ch = 3 * gdn_w
    ab0 = 4 * sba_w + conv_ch
    n_main = w_in.shape[2] - 2 * n_gdn
    gdn_col0 = 4 * sba_w // LANES
    gate_col = (4 * sba_w + conv_ch + gdn_w) // d
    page = cache_k.shape[2]
    ck = jnp.transpose(cache_k, (0, 1, 3, 4, 2)).reshape(cache_k.shape[0], cache_k.shape[1], sba_w, page)
    cv = jnp.transpose(cache_v, (0, 1, 3, 4, 2)).reshape(cache_v.shape[0], cache_v.shape[1], sba_w, page)

    hp = x_prompt.reshape(b * t, d)
    hs = x_sample.reshape(db, d)
    outs = [[] for _ in range(8)]
    for layer in range(depth):
        w_l = w_in[layer]
        w_main = jnp.concatenate([w_l[:, :ab0], w_l[:, ab0 + 2 * n_gdn:]], axis=1).astype(BF16)
        w_ab = jnp.zeros((d, LANES), BF16).at[:, :2 * n_gdn].set(w_l[:, ab0:ab0 + 2 * n_gdn].astype(BF16))
        gain = norm_mix[layer].reshape(1, d)
        wus = w_up_sba[layer].astype(BF16)
        wug = w_up_gdn[layer].astype(BF16)
        wo = w_o[layer].astype(BF16)
        wpg = w_ple_gate[layer].astype(BF16)
        wpp = w_ple_proj[layer].astype(BF16)
        alog_pad = _pad_row(gdn_a_log[layer])
        dtb_pad = _pad_row(gdn_dt_bias[layer])
        gnorm = gdn_norm[layer].reshape(1, GDN_HEAD_DIM)
        nple = norm_ple[layer].reshape(1, d)
        nfin = norm_final.reshape(1, d)
        last = layer == depth - 1

        y_p, gb_p = _norm_matmul(hp, gain, w_main, w_ab, alog_pad, dtb_pad, n_gdn=n_gdn,
                                 tm=_tile(b * t, 1024), tn=_tile(n_main, 1024))
        y3 = y_p.reshape(b, t, n_main)
        ys_p, k_p, v_p = _sba_prompt(y3, sba_bias[layer].astype(F32), n_heads=n_sba)
        yg_p, s_p = _gdn_prompt(y3, gb_p.reshape(b, t, LANES), conv_w[layer], gnorm,
                                n_heads=n_gdn, col0=gdn_col0)
        h_p = _merge(ys_p.reshape(b * t, sba_w), yg_p.reshape(b * t, gdn_w), y_p, hp, wus, wug, wo,
                     tm=_tile(b * t, 256), gate_col=gate_col)
        hp = _ple(h_p, p_prompt[layer].reshape(b * t, -1), nple, nfin, wpg, wpp, tm=_tile(b * t, 512),
                  final_norm=last)
        outs[0].append(k_p.reshape(b, t, n_sba, SBA_HEAD_DIM))
        outs[1].append(v_p.reshape(b, t, n_sba, SBA_HEAD_DIM))
        outs[2].append(y3[:, t - (CONV_W - 1):, 4 * sba_w:4 * sba_w + conv_ch])
        outs[3].append(s_p)

        y_s, gb_s = _norm_matmul(hs, gain, w_main, w_ab, alog_pad, dtb_pad, n_gdn=n_gdn,
                                 tm=db, tn=_tile(n_main, 512))
        ys3 = y_s.reshape(db, 1, n_main)
        ys_s = _sba_sample(ys3, sba_bias[layer].astype(F32).reshape(n_sba, 1), ck, cv, page_table, layer,
                           n_heads=n_sba, pages_per_step=_tile(page_table.shape[1], 16))
        yg_s, cn_s, s_s = _gdn_sample(ys3, gb_s.reshape(db, 1, LANES), state_conv[layer], state_gdn[layer],
                                      conv_w[layer], gnorm, n_heads=n_gdn, col0=4 * sba_w)
        h_s = _merge(ys_s.reshape(db, sba_w), yg_s.reshape(db, gdn_w), y_s, hs, wus, wug, wo,
                     tm=db, gate_col=gate_col)
        hs = _ple(h_s, p_sample[layer].reshape(db, -1), nple, nfin, wpg, wpp, tm=db, final_norm=last)
        outs[4].append(ys3[:, :, sba_w:2 * sba_w].reshape(db, 1, n_sba, SBA_HEAD_DIM))
        outs[5].append(ys3[:, :, 2 * sba_w:3 * sba_w].reshape(db, 1, n_sba, SBA_HEAD_DIM))
        outs[6].append(cn_s)
        outs[7].append(s_s)

    return (hp.reshape(b, t, d), hs.reshape(db, 1, d)) + tuple(jnp.stack(o) for o in outs)
```

```python
import functools

import jax
import jax.numpy as jnp
from jax import lax
from jax.experimental import pallas as pl
from jax.experimental.pallas import tpu as pltpu

F32 = jnp.float32
BF16 = jnp.bfloat16
NORM_EPS = 1e-6
LOG2E = 1.4426950408889634
LANES = 128
SBA_HEAD_DIM = 64
GDN_HEAD_DIM = 128
GDN_CHUNK = 64
GDN_PREP_GROUP = 16
CONV_W = 4
VMEM_LIMIT = 56 * 1024 * 1024


def _tile(n, preferred):
    t = min(preferred, n)
    while n % t:
        t //= 2
    return t


def _softplus(x):
    return jnp.maximum(x, 0.0) + jnp.log1p(jnp.exp(-jnp.abs(x)))


def _sigmoid(x):
    return 0.5 + 0.5 * jnp.tanh(0.5 * x)


def _silu(x):
    h = 0.5 * x
    return h + h * jnp.tanh(h)


def _rms(x, gain):
    ms = jnp.mean(x * x, axis=-1, keepdims=True)
    return x * lax.rsqrt(ms + NORM_EPS) * gain


def _dot(a, b):
    return jnp.dot(a, b, preferred_element_type=F32)


def _dot_nt(a, b):
    return lax.dot_general(a, b, (((1,), (1,)), ((), ())), preferred_element_type=F32)


def _split_hi_lo(x):
    hi = x.astype(BF16)
    lo = (x - hi.astype(F32)).astype(BF16)
    return hi, lo


def _norm_matmul_kernel(x_ref, g_ref, w_ref, wab_ref, alog_ref, dtb_ref, y_ref, gb_ref, u_ref, *, n_gdn):
    @pl.when(pl.program_id(1) == 0)
    def _():
        u = _rms(x_ref[...], g_ref[...]).astype(BF16)
        u_ref[...] = u
        ab = _dot(u, wab_ref[...])
        lane = lax.broadcasted_iota(jnp.int32, ab.shape, 1)
        g = -jnp.exp(alog_ref[...]) * _softplus(ab + dtb_ref[...])
        gb_ref[...] = jnp.where(lane < n_gdn, g, _sigmoid(ab))

    y_ref[...] = _dot(u_ref[...], w_ref[...])


def _norm_matmul(x, gain, w, w_ab, alog_pad, dtb_pad, *, n_gdn, tm, tn):
    m, d = x.shape
    n = w.shape[1]
    assert m % tm == 0 and n % tn == 0
    return pl.pallas_call(
        functools.partial(_norm_matmul_kernel, n_gdn=n_gdn),
        grid=(m // tm, n // tn),
        in_specs=[
            pl.BlockSpec((tm, d), lambda i, j: (i, 0)),
            pl.BlockSpec((1, d), lambda i, j: (0, 0)),
            pl.BlockSpec((d, tn), lambda i, j: (0, j)),
            pl.BlockSpec((d, LANES), lambda i, j: (0, 0)),
            pl.BlockSpec((1, LANES), lambda i, j: (0, 0)),
            pl.BlockSpec((1, LANES), lambda i, j: (0, 0)),
        ],
        out_specs=[
            pl.BlockSpec((tm, tn), lambda i, j: (i, j)),
            pl.BlockSpec((tm, LANES), lambda i, j: (i, 0)),
        ],
        out_shape=[jax.ShapeDtypeStruct((m, n), F32), jax.ShapeDtypeStruct((m, LANES), F32)],
        scratch_shapes=[pltpu.VMEM((tm, d), BF16)],
        compiler_params=pltpu.CompilerParams(
            dimension_semantics=("parallel", "arbitrary"), vmem_limit_bytes=VMEM_LIMIT),
        name="norm_inproj",
    )(x, gain, w, w_ab, alog_pad, dtb_pad)


def _softplus2(y):
    neg_abs = lax.bitcast_convert_type(lax.bitcast_convert_type(y, jnp.uint32) | jnp.uint32(0x80000000), F32)
    return jnp.maximum(y, 0.0) + jnp.log2(1.0 + jnp.exp2(neg_abs))


def _cumsum_total_matrix(n):
    j = lax.broadcasted_iota(jnp.int32, (2 * n, 2 * n), 0)
    j = jnp.where(j >= n, j - n, j)
    s = lax.broadcasted_iota(jnp.int32, (2 * n, 2 * n), 1)
    return ((j >= s) | (s >= n)).astype(BF16)


def _sba_prompt_kernel(bias_ref, q_ref, k_ref, v_ref, gate_ref, o_ref, ko_ref, vo_ref,
                       kb_ref, vb_ref, y_s, w_s, carry_s, acc_s, *, tq, tk, scale):
    hd = SBA_HEAD_DIM
    nh = q_ref.shape[-1] // hd
    r = tq // tk
    g = pl.program_id(1)
    i = pl.program_id(2)

    @pl.when(i == 0)
    def _():
        ko_ref[...] = k_ref[...]
        vo_ref[...] = v_ref[...]
        for h in range(nh):
            kb_ref[h] = k_ref[0, :, h * hd:(h + 1) * hd].astype(BF16)
            vb_ref[h] = v_ref[0, :, h * hd:(h + 1) * hd].astype(BF16)

    q = (q_ref[0] * (scale * LOG2E)).astype(BF16)
    qs = [q[:, h * hd:(h + 1) * hd] for h in range(nh)]
    bias = [bias_ref[g * nh + h] * LOG2E for h in range(nh)]
    cmat = _cumsum_total_matrix(tk)
    row = lax.broadcasted_iota(jnp.int32, (tq, tk), 0)
    col = lax.broadcasted_iota(jnp.int32, (tq, tk), 1)
    q0 = i * tq
    heads = range(nh)
    tiles = [(n, h) for n in range(r) for h in heads]
    tid = {(n, h): n * nh + h for n, h in tiles}

    def starts(grp):
        return [pl.multiple_of(((grp + 1) * r - 1 - n) * tk, tk) for n in range(r)]

    def dead_rows(n):
        return (r - 1 - n) * tk

    def qk_tile(st, slot, t_, lo=0):
        n, h = t_
        y_s[slot, tid[t_], lo:] = _dot_nt(qs[h][lo:], kb_ref[h, pl.ds(st[n], tk), :]) + bias[h]

    def pv_tile(st, t_):
        n, h = t_
        return _dot(w_s[tid[t_]], vb_ref[h, pl.ds(st[n], tk), :])

    def cumsum_tile(y, keep):
        sp = _softplus2(y)
        if keep is not None:
            sp = jnp.where(keep, sp, 0.0)
        return _dot(jnp.concatenate(_split_hi_lo(sp), axis=1), cmat)

    def accumulate(pv):
        for h in heads:
            total = pv[0, h]
            for n in range(1, r):
                total = total + pv[n, h]
            acc_s[h] = acc_s[h] + total

    def weights(y, ct, keep):
        for n in range(r):
            lo = 0 if keep is None else dead_rows(n)
            for h in heads:
                x = jnp.exp2(y[n, h] - ct[n, h][:, :tk] - carry_s[h, lo:])
                w_s[tid[n, h], lo:] = (x if keep is None else jnp.where(keep[n], x, 0.0)).astype(BF16)
                if lo:
                    w_s[tid[n, h], :lo] = jnp.zeros((lo, tk), BF16)
            for h in heads:
                carry_s[h, lo:] = carry_s[h, lo:] + ct[n, h][:, tk:]

    def group(grp, slot, prev, masked):
        st = starts(grp)
        st_next = starts(jnp.maximum(grp - 1, 0))
        st_prev = None if prev is None else starts(prev)
        keep = None
        if masked:
            keep = [((st[n] + col) < (q0 + row))[dead_rows(n):] for n in range(r)]
        y, ct, pv = {}, {}, {}
        for t_ in tiles:
            lo = dead_rows(t_[0]) if masked else 0
            y[t_] = y_s[slot, tid[t_], lo:]
            ct[t_] = cumsum_tile(y[t_], None if keep is None else keep[t_[0]])
            qk_tile(st_next, 1 - slot, t_)
            if prev is not None:
                pv[t_] = pv_tile(st_prev, t_)
        weights(y, ct, keep)
        if prev is not None:
            accumulate(pv)

    carry_s[...] = jnp.zeros_like(carry_s)
    acc_s[...] = jnp.zeros_like(acc_s)
    st_diag = starts(i)
    for t_ in tiles:
        qk_tile(st_diag, 0, t_, dead_rows(t_[0]))
    group(i, 0, None, True)

    def body(t, _):
        group(i - 1 - t, (t + 1) % 2, i - t, False)
        return 0

    lax.fori_loop(0, i, body, 0)
    st_last = starts(0)
    accumulate({t_: pv_tile(st_last, t_) for t_ in tiles})
    o = jnp.concatenate([acc_s[h] for h in heads], axis=1)
    o_ref[0] = (o * _silu(gate_ref[0])).astype(o_ref.dtype)


def _sba_prompt(y3, bias, *, n_heads, tq=256, tk=128, width=256):
    b, t, _ = y3.shape
    width = _tile(n_heads * SBA_HEAD_DIM, width)
    ng = n_heads * SBA_HEAD_DIM // width
    nh = width // SBA_HEAD_DIM
    tq = _tile(t, tq)
    n_tiles = (tq // tk) * nh
    kern = functools.partial(_sba_prompt_kernel, tq=tq, tk=tk, scale=SBA_HEAD_DIM ** -0.5)
    return pl.pallas_call(
        kern,
        grid=(b, ng, t // tq),
        in_specs=[
            pl.BlockSpec(memory_space=pltpu.SMEM),
            pl.BlockSpec((1, tq, width), lambda bi, h, i: (bi, i, h)),
            pl.BlockSpec((1, t, width), lambda bi, h, i: (bi, 0, ng + h)),
            pl.BlockSpec((1, t, width), lambda bi, h, i: (bi, 0, 2 * ng + h)),
            pl.BlockSpec((1, tq, width), lambda bi, h, i: (bi, i, 3 * ng + h)),
        ],
        out_specs=[pl.BlockSpec((1, tq, width), lambda bi, h, i: (bi, i, h)),
                   pl.BlockSpec((1, t, width), lambda bi, h, i: (bi, 0, h)),
                   pl.BlockSpec((1, t, width), lambda bi, h, i: (bi, 0, h))],
        out_shape=[jax.ShapeDtypeStruct((b, t, n_heads * SBA_HEAD_DIM), BF16),
                   jax.ShapeDtypeStruct((b, t, n_heads * SBA_HEAD_DIM), F32),
                   jax.ShapeDtypeStruct((b, t, n_heads * SBA_HEAD_DIM), F32)],
        scratch_shapes=[pltpu.VMEM((nh, t, SBA_HEAD_DIM), BF16), pltpu.VMEM((nh, t, SBA_HEAD_DIM), BF16),
                        pltpu.VMEM((2, n_tiles, tq, tk), F32), pltpu.VMEM((n_tiles, tq, tk), BF16),
                        pltpu.VMEM((nh, tq, tk), F32), pltpu.VMEM((nh, tq, SBA_HEAD_DIM), F32)],
        compiler_params=pltpu.CompilerParams(
            dimension_semantics=("parallel", "parallel", "arbitrary"), vmem_limit_bytes=VMEM_LIMIT),
        name="sba_prompt",
    )(bias, y3, y3, y3, y3)


def _sba_sample_kernel(pt_ref, q_ref, kn_ref, vn_ref, gate_ref, bias_ref, *rest, pages_per_step, past, scale):
    del pt_ref
    pp = pages_per_step
    k_refs = rest[:pp]
    v_refs = rest[pp:2 * pp]
    o_ref, oacc_ref, carry_ref, qbd_ref = rest[2 * pp:]
    hd = SBA_HEAD_DIM
    nh, width = qbd_ref.shape
    page = k_refs[0].shape[1]
    j = pl.program_id(1)
    head_of_lane = lax.broadcasted_iota(jnp.int32, (nh, width), 1) // hd
    own = head_of_lane == lax.broadcasted_iota(jnp.int32, (nh, width), 0)

    @pl.when(j == 0)
    def _():
        qrow = q_ref[0] * scale
        qbd_ref[...] = jnp.where(own, jnp.broadcast_to(qrow, (nh, width)), 0.0).astype(BF16)
        oacc_ref[...] = jnp.zeros_like(oacc_ref)
        carry_ref[...] = jnp.zeros_like(carry_ref)

    cmat = _cumsum_total_matrix(page)
    qbd = qbd_ref[...]
    bias = bias_ref[...]
    carry = carry_ref[...]
    slots = range(pp)
    z = [_dot(qbd, k_refs[s][...].astype(BF16)) + bias for s in slots]
    sp = [_softplus(z[s]) for s in slots]
    ct = [_dot(jnp.concatenate(_split_hi_lo(sp[s]), axis=1), cmat) for s in slots]
    w = []
    for s in slots:
        w.append(jnp.exp(z[s] - ct[s][:, :page] - carry).astype(BF16))
        carry = carry + ct[s][:, page:]
    pv = [_dot_nt(w[s], v_refs[s][...].astype(BF16)) for s in slots]
    oacc = oacc_ref[...]
    for s in slots:
        oacc = oacc + pv[s]
    oacc_ref[...] = oacc
    carry_ref[...] = carry

    @pl.when(j == pl.num_programs(1) - 1)
    def _():
        o = jnp.sum(jnp.where(own, oacc, 0.0), axis=0, keepdims=True)
        kn = kn_ref[0].astype(BF16).astype(F32)
        z_new = jnp.sum(qbd.astype(F32) * kn, axis=-1, keepdims=True) + bias
        q_pos = jnp.full(z_new.shape, past, jnp.int32)
        k_pos = jnp.full(z_new.shape, past, jnp.int32)
        w_new = jnp.where(k_pos < q_pos, jnp.exp(z_new - _softplus(z_new)), 0.0)
        w_new = w_new.astype(BF16).astype(F32)
        w_lane = jnp.sum(jnp.where(own, jnp.broadcast_to(w_new, (nh, width)), 0.0), axis=0, keepdims=True)
        o = o + w_lane * vn_ref[0].astype(BF16).astype(F32)
        o_ref[0] = (o * _silu(gate_ref[0])).astype(o_ref.dtype)


def _sba_sample(ys3, bias_col, cache_k, cache_v, page_table, layer, *, n_heads, pages_per_step=8):
    db = ys3.shape[0]
    width = n_heads * SBA_HEAD_DIM
    n_pages = page_table.shape[1]
    page = cache_k.shape[3]
    pp = pages_per_step
    assert n_pages % pp == 0
    kern = functools.partial(_sba_sample_kernel, pages_per_step=pp, past=n_pages * page,
                             scale=SBA_HEAD_DIM ** -0.5)

    def row_spec(col):
        return pl.BlockSpec((1, 1, width), lambda b, j, pt: (b, 0, col))

    def page_spec(s):
        return pl.BlockSpec((None, None, width, page),
                            lambda b, j, pt: (layer, pt[b, n_pages - 1 - (j * pp + s)], 0, 0))

    grid_spec = pltpu.PrefetchScalarGridSpec(
        num_scalar_prefetch=1,
        grid=(db, n_pages // pp),
        in_specs=[row_spec(0), row_spec(1), row_spec(2), row_spec(3),
                  pl.BlockSpec((n_heads, 1), lambda b, j, pt: (0, 0))]
        + [page_spec(s) for s in range(pp)] + [page_spec(s) for s in range(pp)],
        out_specs=pl.BlockSpec((1, 1, width), lambda b, j, pt: (b, 0, 0)),
        scratch_shapes=[pltpu.VMEM((n_heads, width), F32), pltpu.VMEM((n_heads, page), F32),
                        pltpu.VMEM((n_heads, width), BF16)],
    )
    return pl.pallas_call(
        kern,
        grid_spec=grid_spec,
        out_shape=jax.ShapeDtypeStruct((db, 1, width), BF16),
        compiler_params=pltpu.CompilerParams(
            dimension_semantics=("parallel", "arbitrary"), vmem_limit_bytes=VMEM_LIMIT),
        name="sba_sample",
    )(page_table, ys3, ys3, ys3, ys3, bias_col, *([cache_k] * pp), *([cache_v] * pp))


def _bdot_nt(a, b):
    return jnp.einsum('gmk,gnk->gmn', a, b, preferred_element_type=F32)


def _bdot_split(a, b):
    a_hi, a_lo = _split_hi_lo(a)
    b_hi, b_lo = _split_hi_lo(b)
    lhs = jnp.concatenate([a_hi, a_lo, a_hi], axis=2)
    rhs = jnp.concatenate([b_hi, b_hi, b_lo], axis=1)
    return jnp.einsum('gmk,gkn->gmn', lhs, rhs, preferred_element_type=F32)


def _unit_lower_inverse(a):
    n = a.shape[-1]
    row = lax.broadcasted_iota(jnp.int32, (n, 2 * n), 0)
    col = lax.broadcasted_iota(jnp.int32, (n, 2 * n), 1)
    right = col >= n
    p = jnp.concatenate([-a, jnp.zeros_like(a)], axis=2) + (col - n == row).astype(F32)
    span = 1
    while span < n:
        p = _bdot_split(p[:, :, :n], p) + jnp.where(right, p, 0.0)
        span *= 2
    return p[:, :, n:]


def _gdn_prompt_kernel(xq_ref, xk_ref, xv_ref, zg_ref, gb_ref, cwq_ref, cwk_ref, cwv_ref, gn_ref, y_ref, s_ref,
                       k_s, q_s, kb_s, vb_s, g_s, gq_s, n_s, op_s, eg_s, o_s, *, n_heads):
    c = GDN_CHUNK
    dh = GDN_HEAD_DIM
    hps = xq_ref.shape[2] // dh
    h0 = pl.program_id(1) * hps
    t = xq_ref.shape[1]
    nc = t // c
    group = _tile(nc, GDN_PREP_GROUP)
    rows = lax.broadcasted_iota(jnp.int32, (t, dh), 0)
    lane = lax.broadcasted_iota(jnp.int32, (t, dh), 1)
    ri = lax.broadcasted_iota(jnp.int32, (c, c), 0)
    ci = lax.broadcasted_iota(jnp.int32, (c, c), 1)
    causal = ri >= ci
    strict = ri > ci
    diag = ri == ci
    tril3 = jnp.broadcast_to(jnp.concatenate([causal.astype(BF16)] * 3, axis=1), (group, c, 3 * c))

    def l2n(x):
        return x * lax.rsqrt(jnp.sum(x * x, axis=-1, keepdims=True) + NORM_EPS)

    for hh in range(hps):
        lanes = slice(hh * dh, (hh + 1) * dh)

        def conv_silu(x_ref, cw_ref):
            x = x_ref[0, :, lanes]
            cw = cw_ref[:, lanes]
            y = None
            for tap in range(CONV_W):
                shift = CONV_W - 1 - tap
                xs = x if shift == 0 else jnp.where(rows >= shift, pltpu.roll(x, shift, axis=0), 0.0)
                term = xs * cw[tap:tap + 1, :]
                y = term if y is None else y + term
            return _silu(y)

        q = l2n(conv_silu(xq_ref, cwq_ref)) * (dh ** -0.5)
        k = l2n(conv_silu(xk_ref, cwk_ref))
        v = conv_silu(xv_ref, cwv_ref)

        gb = gb_ref[0]
        g = jnp.sum(jnp.where(lane == h0 + hh, gb, 0.0), axis=-1, keepdims=True)
        beta = jnp.sum(jnp.where(lane == n_heads + h0 + hh, gb, 0.0), axis=-1, keepdims=True)

        k_s[...] = k.reshape(nc, c, dh)
        q_s[...] = q.reshape(nc, c, dh)
        kb_s[...] = (k * beta).reshape(nc, c, dh)
        vb_s[...] = (v * beta).reshape(nc, c, dh)
        g_s[...] = jnp.broadcast_to(g, (t, dh)).reshape(nc, c, dh)

        def prep(n, _):
            sl = pl.ds(pl.multiple_of(n * group, group), group)
            k_c = k_s[sl]
            q_c = q_s[sl]
            kb_c = kb_s[sl]
            g_c = g_s[sl]
            g_hi = g_c.astype(BF16)
            g_rest = g_c - g_hi.astype(F32)
            g_mid, g_lo = _split_hi_lo(g_rest)
            cg_c = jnp.einsum('gmk,gkn->gmn', tril3, jnp.concatenate([g_hi, g_mid, g_lo], axis=1),
                              preferred_element_type=F32)
            cg_col = cg_c[:, :, :c]
            cg_row = jnp.sum(jnp.where(diag, cg_col, 0.0), axis=1, keepdims=True)
            decay = jnp.where(causal, jnp.exp(cg_col - cg_row), 0.0)
            kbq = _bdot_nt(jnp.concatenate([kb_c, q_c], axis=1).astype(BF16), k_c.astype(BF16))
            a_mat = jnp.where(strict, kbq[:, :c] * decay, 0.0)
            attn = jnp.where(causal, kbq[:, c:] * decay, 0.0)
            tinv = _unit_lower_inverse(a_mat)
            ecg = jnp.exp(cg_c)
            g_last = cg_c[:, c - 1:c, :]
            uw = _bdot_split(tinv, jnp.concatenate([vb_s[sl], kb_c * ecg], axis=2))
            u_hi, u_lo = _split_hi_lo(uw[:, :, :dh])
            rhs = jnp.concatenate([uw[:, :, dh:].astype(BF16), u_hi, u_lo], axis=2)
            kd_t = jnp.swapaxes(k_c * jnp.exp(g_last - cg_c), 1, 2).astype(BF16)
            kar = jnp.einsum('gmc,gcn->gmn', jnp.concatenate([kd_t, attn.astype(BF16)], axis=1), rhs,
                             preferred_element_type=F32)
            kr = kar[:, :dh]
            ar = kar[:, dh:]
            gq = jnp.concatenate([kr[:, :, :dh], q_c * ecg - ar[:, :, :dh]], axis=1)
            gq_s[hh, sl] = jnp.concatenate(_split_hi_lo(gq), axis=2)
            n_s[hh, sl] = kr[:, :, dh:2 * dh] + kr[:, :, 2 * dh:]
            op_s[hh, sl] = ar[:, :, dh:2 * dh] + ar[:, :, 2 * dh:]
            eg_s[hh, sl] = jnp.exp(g_last)
            return 0

        lax.fori_loop(0, nc // group, prep, 0)

    def scan(n, states):
        sb = [s.astype(BF16) for s in states]
        gs_qs = [_dot(gq_s[hh, n], jnp.concatenate([sb[hh], sb[hh]], axis=0)) for hh in range(hps)]
        for hh in range(hps):
            o_s[hh, n] = gs_qs[hh][dh:] + op_s[hh, n]
        return tuple(states[hh] * eg_s[hh, n] + n_s[hh, n] - gs_qs[hh][:dh] for hh in range(hps))

    s_fin = lax.fori_loop(0, nc, scan, tuple(jnp.zeros((dh, dh), F32) for _ in range(hps)))
    for hh in range(hps):
        lanes = slice(hh * dh, (hh + 1) * dh)
        s_ref[0, hh] = s_fin[hh]
        o = o_s[hh].reshape(t, dh)
        y_ref[0, :, lanes] = (_rms(o, gn_ref[...]) * _silu(zg_ref[0, :, lanes])).astype(y_ref.dtype)


def _gdn_prompt(y3, gb3, conv_w, gdn_norm, *, n_heads, col0, heads_per_step=2):
    b, t, _ = y3.shape
    dh = GDN_HEAD_DIM
    c = GDN_CHUNK
    nc = t // c
    hps = heads_per_step
    assert n_heads % hps == 0 and col0 % hps == 0
    wide = hps * dh
    kern = functools.partial(_gdn_prompt_kernel, n_heads=n_heads)

    def seq_spec(off):
        return pl.BlockSpec((1, t, wide), lambda bi, h: (bi, 0, off // hps + h))

    def cw_spec(off):
        return pl.BlockSpec((CONV_W, wide), lambda bi, h: (0, off // hps + h))

    small = pl.BlockSpec((1, LANES), lambda bi, h: (0, 0))
    chunk_f32 = pltpu.VMEM((nc, c, dh), F32)
    return pl.pallas_call(
        kern,
        grid=(b, n_heads // hps),
        in_specs=[seq_spec(col0), seq_spec(col0 + n_heads), seq_spec(col0 + 2 * n_heads),
                  seq_spec(col0 + 3 * n_heads),
                  pl.BlockSpec((1, t, LANES), lambda bi, h: (bi, 0, 0)),
                  cw_spec(0), cw_spec(n_heads), cw_spec(2 * n_heads), small],
        out_specs=[pl.BlockSpec((1, t, wide), lambda bi, h: (bi, 0, h)),
                   pl.BlockSpec((1, hps, dh, dh), lambda bi, h: (bi, h, 0, 0))],
        out_shape=[jax.ShapeDtypeStruct((b, t, n_heads * dh), BF16),
                   jax.ShapeDtypeStruct((b, n_heads, dh, dh), F32)],
        scratch_shapes=[chunk_f32, chunk_f32, chunk_f32, chunk_f32, chunk_f32,
                        pltpu.VMEM((hps, nc, dh + c, 2 * dh), BF16),
                        pltpu.VMEM((hps, nc, dh, dh), F32),
                        pltpu.VMEM((hps, nc, c, dh), F32),
                        pltpu.VMEM((hps, nc, 1, dh), F32),
                        pltpu.VMEM((hps, nc, c, dh), F32)],
        compiler_params=pltpu.CompilerParams(
            dimension_semantics=("parallel", "parallel"), vmem_limit_bytes=VMEM_LIMIT),
        name="gdn_prompt",
    )(y3, y3, y3, y3, gb3, conv_w, conv_w, conv_w, gdn_norm)


def _gdn_sample_kernel(y_ref, gb_ref, cs_ref, st_ref, cw_ref, gn_ref, yo_ref, cn_ref, so_ref, *, n_heads, col0):
    dh = GDN_HEAD_DIM
    kw = n_heads * dh
    row = y_ref[0]
    xg = row[:, col0:col0 + 3 * kw]
    zg = row[:, col0 + 3 * kw:col0 + 4 * kw]
    buf = cs_ref[0]
    cw = cw_ref[...]
    yc = buf[0:1] * cw[0:1]
    for tap in range(1, CONV_W - 1):
        yc = yc + buf[tap:tap + 1] * cw[tap:tap + 1]
    yc = yc + xg * cw[CONV_W - 1:CONV_W]
    act = _silu(yc)
    for tap in range(CONV_W - 2):
        cn_ref[0, tap:tap + 1, :] = buf[tap + 1:tap + 2]
    cn_ref[0, CONV_W - 2:CONV_W - 1, :] = xg

    g_all = gb_ref[0]
    beta_all = g_all
    eye =lax.broadcasted_iota(jnp.int32, (dh, dh), 0) == lax.broadcasted_iota(jnp.int32, (dh, dh), 1)

    def to_col(r):
        return jnp.sum(jnp.where(eye, jnp.broadcast_to(r, (dh, dh)), 0.0), axis=1, keepdims=True)

    def l2n(x):
        return x * lax.rsqrt(jnp.sum(x * x, axis=-1, keepdims=True) + NORM_EPS)

    gn = gn_ref[...]
    for h in range(n_heads):
        qh = l2n(act[:, h * dh:(h + 1) * dh]) * (dh ** -0.5)
        kh = l2n(act[:, kw + h * dh:kw + (h + 1) * dh])
        vh = act[:, 2 * kw + h * dh:2 * kw + (h + 1) * dh]
        g_h = g_all[:, h:h + 1]
        beta_h = beta_all[:, n_heads + h:n_heads + h + 1]
        s = st_ref[0, h] * jnp.exp(g_h)
        kcol = to_col(kh)
        ks = jnp.sum(kcol * s, axis=0, keepdims=True)
        delta = beta_h * (vh - ks)
        s = s + kcol * delta
        so_ref[0, h] = s
        o = jnp.sum(to_col(qh) * s, axis=0, keepdims=True)
        yo_ref[0, :, h * dh:(h + 1) * dh] = (
            _rms(o, gn) * _silu(zg[:, h * dh:(h + 1) * dh])).astype(yo_ref.dtype)


def _gdn_sample(ys3, gb3, conv_state, gdn_state, conv_w, gdn_norm, *, n_heads, col0):
    db, _, n = ys3.shape
    dh = GDN_HEAD_DIM
    kw = n_heads * dh
    kern = functools.partial(_gdn_sample_kernel, n_heads=n_heads, col0=col0)
    small = pl.BlockSpec((1, LANES), lambda b: (0, 0))
    return pl.pallas_call(
        kern,
        grid=(db,),
        in_specs=[pl.BlockSpec((1, 1, n), lambda b: (b, 0, 0)),
                  pl.BlockSpec((1, 1, LANES), lambda b: (b, 0, 0)),
                  pl.BlockSpec((1, CONV_W - 1, 3 * kw), lambda b: (b, 0, 0)),
                  pl.BlockSpec((1, n_heads, dh, dh), lambda b: (b, 0, 0, 0)),
                  pl.BlockSpec((CONV_W, 3 * kw), lambda b: (0, 0)),
                  small],
        out_specs=[pl.BlockSpec((1, 1, kw), lambda b: (b, 0, 0)),
                   pl.BlockSpec((1, CONV_W - 1, 3 * kw), lambda b: (b, 0, 0)),
                   pl.BlockSpec((1, n_heads, dh, dh), lambda b: (b, 0, 0, 0))],
        out_shape=[jax.ShapeDtypeStruct((db, 1, kw), BF16),
                   jax.ShapeDtypeStruct((db, CONV_W - 1, 3 * kw), F32),
                   jax.ShapeDtypeStruct((db, n_heads, dh, dh), F32)],
        compiler_params=pltpu.CompilerParams(
            dimension_semantics=("parallel",), vmem_limit_bytes=VMEM_LIMIT),
        name="gdn_sample",
    )(ys3, gb3, conv_state, gdn_state, conv_w, gdn_norm)


def _merge_kernel(ys_ref, yg_ref, ms_ref, mg_ref, x_ref, wus_ref, wug_ref, wo_ref, h_ref):
    merged = (_sigmoid(ms_ref[...]) * _dot(ys_ref[...], wus_ref[...])
              + _sigmoid(mg_ref[...]) * _dot(yg_ref[...], wug_ref[...]))
    h_ref[...] = x_ref[...] + _dot(merged.astype(BF16), wo_ref[...])


def _merge(ys, yg, y, x, w_up_sba, w_up_gdn, w_o, *, tm, gate_col):
    m, d = x.shape
    kw = ys.shape[1]

    def resident(shape):
        return pl.BlockSpec(shape, lambda i: (0, 0), pipeline_mode=pl.Buffered(1))

    return pl.pallas_call(
        _merge_kernel,
        grid=(m // tm,),
        in_specs=[pl.BlockSpec((tm, kw), lambda i: (i, 0)),
                  pl.BlockSpec((tm, kw), lambda i: (i, 0)),
                  pl.BlockSpec((tm, d), lambda i: (i, gate_col)),
                  pl.BlockSpec((tm, d), lambda i: (i, gate_col + 1)),
                  pl.BlockSpec((tm, d), lambda i: (i, 0)),
                  resident(w_up_sba.shape), resident(w_up_gdn.shape), resident(w_o.shape)],
        out_specs=pl.BlockSpec((tm, d), lambda i: (i, 0)),
        out_shape=jax.ShapeDtypeStruct((m, d), F32),
        compiler_params=pltpu.CompilerParams(
            dimension_semantics=("parallel",), vmem_limit_bytes=VMEM_LIMIT),
        name="merge_out",
    )(ys, yg, y, y, x, w_up_sba, w_up_gdn, w_o)


def _ple_kernel(h_ref, p_ref, np_ref, nf_ref, wg_ref, wp_ref, o_ref, *, final_norm):
    h = h_ref[...]
    gate = _sigmoid(_dot(_rms(h, np_ref[...]).astype(BF16), wg_ref[...]))
    h = h + gate * _dot(p_ref[...].astype(BF16), wp_ref[...])
    o_ref[...] = _rms(h, nf_ref[...]) if final_norm else h


def _ple(h, p, norm_ple, norm_final, w_gate, w_proj, *, tm, final_norm):
    m, d = h.shape
    pd = p.shape[1]

    def resident(shape):
        return pl.BlockSpec(shape, lambda i: (0, 0), pipeline_mode=pl.Buffered(1))

    return pl.pallas_call(
        functools.partial(_ple_kernel, final_norm=final_norm),
        grid=(m // tm,),
        in_specs=[pl.BlockSpec((tm, d), lambda i: (i, 0)),
                  pl.BlockSpec((tm, pd), lambda i: (i, 0)),
                  resident((1, d)), resident((1, d)), resident(w_gate.shape), resident(w_proj.shape)],
        out_specs=pl.BlockSpec((tm, d), lambda i: (i, 0)),
        out_shape=jax.ShapeDtypeStruct((m, d), F32),
        compiler_params=pltpu.CompilerParams(
            dimension_semantics=("parallel",), vmem_limit_bytes=VMEM_LIMIT),
        name="ple_norm",
    )(h, p, norm_ple, norm_final, w_gate, w_proj)


def _pad_row(v):
    return jnp.zeros((1, LANES), F32).at[0, :v.shape[0]].set(v.astype(F32))


def kernel(x_prompt, x_sample, cache_k, cache_v, state_conv, state_gdn, page_table, p_prompt, p_sample, norm_mix, w_in, sba_bias, conv_w, gdn_a_log, gdn_dt_bias, gdn_norm, w_up_sba, w_up_gdn, w_o, norm_ple, w_ple_gate, w_ple_proj, norm_final):
    depth = w_in.shape[0]
    b, t, d = x_prompt.shape
    db, dt, _ = x_sample.shape
    assert dt == 1
    n_sba = sba_bias.shape[1]
    n_gdn = gdn_a_log.shape[1]
    sba_w = n_sba * SBA_HEAD_DIM
    gdn_w = n_gdn * GDN_HEAD_DIM
    conv_ch = 3 * gdn_w
    ab0 = 4 * sba_w + conv_ch
    n_main = w_in.shape[2] - 2 * n_gdn
    gdn_col0 = 4 * sba_w // LANES
    gate_col = (4 * sba_w + conv_ch + gdn_w) // d
    page = cache_k.shape[2]
    ck = jnp.transpose(cache_k, (0, 1, 3, 4, 2)).reshape(cache_k.shape[0], cache_k.shape[1], sba_w, page)
    cv = jnp.transpose(cache_v, (0, 1, 3, 4, 2)).reshape(cache_v.shape[0], cache_v.shape[1], sba_w, page)

    hp = x_prompt.reshape(b * t, d)
    hs = x_sample.reshape(db, d)
    outs = [[] for _ in range(8)]
    for layer in range(depth):
        w_l = w_in[layer]
        w_main = jnp.concatenate([w_l[:, :ab0], w_l[:, ab0 + 2 * n_gdn:]], axis=1).astype(BF16)
        w_ab = jnp.zeros((d, LANES), BF16).at[:, :2 * n_gdn].set(w_l[:, ab0:ab0 + 2 * n_gdn].astype(BF16))
        gain = norm_mix[layer].reshape(1, d)
        wus = w_up_sba[layer].astype(BF16)
        wug = w_up_gdn[layer].astype(BF16)
        wo = w_o[layer].astype(BF16)
        wpg = w_ple_gate[layer].astype(BF16)
        wpp = w_ple_proj[layer].astype(BF16)
        alog_pad = _pad_row(gdn_a_log[layer])
        dtb_pad = _pad_row(gdn_dt_bias[layer])
        gnorm = gdn_norm[layer].reshape(1, GDN_HEAD_DIM)
        nple = norm_ple[layer].reshape(1, d)
        nfin = norm_final.reshape(1, d)
        last = layer == depth - 1

        y_p, gb_p = _norm_matmul(hp, gain, w_main, w_ab, alog_pad, dtb_pad, n_gdn=n_gdn,
                                 tm=_tile(b * t, 1024), tn=_tile(n_main, 1024))
        y3 = y_p.reshape(b, t, n_main)
        ys_p, k_p, v_p = _sba_prompt(y3, sba_bias[layer].astype(F32), n_heads=n_sba)
        yg_p, s_p = _gdn_prompt(y3, gb_p.reshape(b, t, LANES), conv_w[layer], gnorm,
                                n_heads=n_gdn, col0=gdn_col0)
        h_p = _merge(ys_p.reshape(b * t, sba_w), yg_p.reshape(b * t, gdn_w), y_p, hp, wus, wug, wo,
                     tm=_tile(b * t, 256), gate_col=gate_col)
        hp = _ple(h_p, p_prompt[layer].reshape(b * t, -1), nple, nfin, wpg, wpp, tm=_tile(b * t, 512),
                  final_norm=last)
        outs[0].append(k_p.reshape(b, t, n_sba, SBA_HEAD_DIM))
        outs[1].append(v_p.reshape(b, t, n_sba, SBA_HEAD_DIM))
        outs[2].append(y3[:, t - (CONV_W - 1):, 4 * sba_w:4 * sba_w + conv_ch])
        outs[3].append(s_p)

        y_s, gb_s = _norm_matmul(hs, gain, w_main, w_ab, alog_pad, dtb_pad, n_gdn=n_gdn,
                                 tm=db, tn=_tile(n_main, 512))
        ys3 = y_s.reshape(db, 1, n_main)
        ys_s = _sba_sample(ys3, sba_bias[layer].astype(F32).reshape(n_sba, 1), ck, cv, page_table, layer,
                           n_heads=n_sba, pages_per_step=_tile(page_table.shape[1], 16))
        yg_s, cn_s, s_s = _gdn_sample(ys3, gb_s.reshape(db, 1, LANES), state_conv[layer], state_gdn[layer],
                                      conv_w[layer], gnorm, n_heads=n_gdn, col0=4 * sba_w)
        h_s = _merge(ys_s.reshape(db, sba_w), yg_s.reshape(db, gdn_w), y_s, hs, wus, wug, wo,
                     tm=db, gate_col=gate_col)
        hs = _ple(h_s, p_sample[layer].reshape(db, -1), nple, nfin, wpg, wpp, tm=db, final_norm=last)
        outs[4].append(ys3[:, :, sba_w:2 * sba_w].reshape(db, 1, n_sba, SBA_HEAD_DIM))
        outs[5].append(ys3[:, :, 2 * sba_w:3 * sba_w].reshape(db, 1, n_sba, SBA_HEAD_DIM))
        outs[6].append(cn_s)
        outs[7].append(s_s)

    return (hp.reshape(b, t, d), hs.reshape(db, 1, d)) + tuple(jnp.stack(o) for o in outs)
```
